```python
import jax, jax.numpy as jnp
from jax import lax
import numpy as np

D_MODEL = 1024
BATCH = 1
SEQ = 16384
DEPTH = 2
DEC_BATCH = 8
DEC_SEQ = 8192
PAST_LEN = 128

N_MIXERS = 2
N_RET_LAYERS = (DEPTH + 1) // 2
N_GM_LAYERS = DEPTH // 2
RET_HEADS = 4
RET_DK = D_MODEL // RET_HEADS
RET_DV = 2 * RET_DK
RET_QK = RET_HEADS * RET_DK
RET_VDIM = RET_HEADS * RET_DV
RET_IN = 2 * RET_QK + 2 * RET_VDIM
RET_CHUNK = 128
ROPE_BASE = 10000.0
GM_GROUPS = 4
GM_WIDTH = 2 * D_MODEL
GM_GDIM = GM_WIDTH // GM_GROUPS
GM_CHUNK = 128
N_EXPERTS = 32
TOP_K = 4
D_EXPERT = D_MODEL
SWIGLU_LIMIT = 7.0
SWIGLU_ALPHA = 1.702
MOE_BLOCK = 256
N_MOD = 6
EPS = 1e-6

kernel_name = "hybrid_retention_gmlp_moe_encoder"


def _rmsnorm(x, g):
    xf = x.astype(jnp.float32)
    y = xf * lax.rsqrt(jnp.mean(xf * xf, axis=-1, keepdims=True) + EPS) * g.astype(jnp.float32)
    return y.astype(x.dtype)


def _layernorm(x, g, b):
    xf = x.astype(jnp.float32)
    mu = jnp.mean(xf, axis=-1, keepdims=True)
    var = jnp.mean(jnp.square(xf - mu), axis=-1, keepdims=True)
    y = (xf - mu) * lax.rsqrt(var + EPS) * g.astype(jnp.float32) + b.astype(jnp.float32)
    return y.astype(x.dtype)


def _rope(x):
    S, D = x.shape[1], x.shape[-1]
    half = D // 2
    inv = ROPE_BASE ** (-jnp.arange(half, dtype=jnp.float32) / half)
    ang = jnp.arange(S, dtype=jnp.float32)[:, None] * inv[None, :]
    cos = jnp.cos(ang)[None, :, None, :]
    sin = jnp.sin(ang)[None, :, None, :]
    x1, x2 = x[..., :half], x[..., half:]
    return jnp.concatenate([x1 * cos - x2 * sin, x2 * cos + x1 * sin], axis=-1)


def _retention_direction(q, k, v, log_gamma, include_diag):
    B, S, H, DK = q.shape
    DV = v.shape[-1]
    nc = S // RET_CHUNK
    idx = jnp.arange(RET_CHUNK, dtype=jnp.float32)
    diff = idx[:, None] - idx[None, :]
    tri = (diff >= 0) if include_diag else (diff > 0)
    decay_intra = jnp.where(tri[None], jnp.exp(log_gamma[:, None, None] * jnp.where(tri, diff, 0.0)[None]), 0.0)
    q_decay = jnp.exp(log_gamma[None, :] * (idx + 1.0)[:, None])
    k_decay = jnp.exp(log_gamma[None, :] * (RET_CHUNK - 1.0 - idx)[:, None])
    chunk_decay = jnp.exp(log_gamma * RET_CHUNK)

    def to_chunks(t):
        return jnp.moveaxis(t.reshape(B, nc, RET_CHUNK, H, t.shape[-1]), 1, 0)

    def step(state, blk):
        qc, kc, vc = blk
        s = jnp.einsum('bihd,bjhd->bhij', qc, kc) * decay_intra
        o = (jnp.einsum('bhij,bjhe->bihe', s, vc)
             + jnp.einsum('bihd,bhde->bihe', qc, state) * q_decay[None, :, :, None])
        state = (state * chunk_decay[None, :, None, None]
                 + jnp.einsum('bjhd,bjhe->bhde', kc * k_decay[None, :, :, None], vc))
        return state, o

    state0 = jnp.zeros((B, H, DK, DV), jnp.float32)
    _, o = lax.scan(step, state0, (to_chunks(q), to_chunks(k), to_chunks(v)))
    return jnp.moveaxis(o, 0, 1).reshape(B, S, H, DV)


def _retention_mixer(h, w_in, log_decay, gn_g, w_out):
    B, S, _ = h.shape
    proj = h @ w_in
    q, k, v, g = jnp.split(proj, [RET_QK, 2 * RET_QK, 2 * RET_QK + RET_VDIM], axis=-1)
    q = _rope(q.reshape(B, S, RET_HEADS, RET_DK).astype(jnp.float32)) * (RET_DK ** -0.5)
    k = _rope(k.reshape(B, S, RET_HEADS, RET_DK).astype(jnp.float32))
    v = v.reshape(B, S, RET_HEADS, RET_DV).astype(jnp.float32)
    ld = log_decay.astype(jnp.float32)
    fwd = _retention_direction(q, k, v, ld[0], True)
    bwd = _retention_direction(q[:, ::-1], k[:, ::-1], v[:, ::-1], ld[1], False)[:, ::-1]
    o = fwd + bwd
    mu = jnp.mean(o, axis=-1, keepdims=True)
    var = jnp.mean(jnp.square(o - mu), axis=-1, keepdims=True)
    o = ((o - mu) * lax.rsqrt(var + EPS)).reshape(B, S, RET_VDIM) * gn_g.astype(jnp.float32)
    o = o.astype(h.dtype) * jax.nn.silu(g)
    return o @ w_out


def _gmlp_mixer(h, w_in, b_in, vn_g, vn_b, w_s, b_s, w_out, b_out):
    B, S, _ = h.shape
    z = jax.nn.gelu(h @ w_in + b_in, approximate=False)
    u, v = jnp.split(z, 2, axis=-1)
    v = _layernorm(v, vn_g, vn_b)
    nc = S // GM_CHUNK
    vc = v.reshape(B, nc, GM_CHUNK, GM_GROUPS, GM_GDIM)
    mixed = jnp.einsum('gij,bnjgc->bnigc', w_s, vc) + b_s.T[None, None, :, :, None]
    return (u * mixed.reshape(B, S, GM_WIDTH)) @ w_out + b_out


def _moe(h, w_r, b_r, w_gu, b_gu, w_dn, b_dn):
    T, D = h.shape
    logits = (h @ w_r + b_r).astype(jnp.float32)
    top_v, top_i = lax.top_k(logits, TOP_K)
    gates = jax.nn.softmax(top_v, axis=-1).astype(h.dtype)
    A = T * TOP_K
    flat_e = top_i.reshape(A)
    flat_tok = jnp.repeat(jnp.arange(T, dtype=jnp.int32), TOP_K)
    flat_g = gates.reshape(A)
    order = jnp.argsort(flat_e)
    se = flat_e[order]
    counts = jnp.bincount(flat_e, length=N_EXPERTS)
    padded = (counts + MOE_BLOCK - 1) // MOE_BLOCK * MOE_BLOCK
    start = jnp.cumsum(counts) - counts
    pend = jnp.cumsum(padded)
    pstart = pend - padded
    dest = pstart[se] + (jnp.arange(A, dtype=jnp.int32) - start[se])
    n_blocks = -(-A // MOE_BLOCK) + N_EXPERTS
    P = n_blocks * MOE_BLOCK
    row_tok = jnp.full((P,), T, jnp.int32).at[dest].set(flat_tok[order])
    row_gate = jnp.zeros((P,), h.dtype).at[dest].set(flat_g[order])
    block_e = jnp.minimum(jnp.searchsorted(pend, jnp.arange(n_blocks, dtype=jnp.int32) * MOE_BLOCK, side='right'),
                          N_EXPERTS - 1).astype(jnp.int32)
    h_pad = jnp.concatenate([h, jnp.zeros((1, D), h.dtype)], axis=0)
    xb = h_pad[row_tok].reshape(n_blocks, MOE_BLOCK, D)

    def expert_block(args):
        xblk, e = args
        hg = xblk @ w_gu[e] + b_gu[e]
        glu = jnp.minimum(hg[:, ::2], SWIGLU_LIMIT)
        lin = jnp.clip(hg[:, 1::2], -SWIGLU_LIMIT, SWIGLU_LIMIT)
        act = glu * jax.nn.sigmoid(SWIGLU_ALPHA * glu) * (lin + 1.0)
        return act @ w_dn[e] + b_dn[e]

    yb = lax.map(expert_block, (xb, block_e)).reshape(P, D) * row_gate[:, None]
    return jax.ops.segment_sum(yb, row_tok, num_segments=T + 1)[:T]


def _trunk(x, c, ada_w, ada_b, norm1_g, norm2_g,
           ret_w_in, ret_log_decay, ret_gn_g, ret_w_out,
           gm_w_in, gm_b_in, gm_vn_g, gm_vn_b, gm_w_s, gm_b_s, gm_w_out, gm_b_out,
           moe_w_r, moe_b_r, moe_w_gu, moe_b_gu, moe_w_dn, moe_b_dn, final_g):
    B, S, D = x.shape
    c_act = jax.nn.silu(c)
    for i in range(DEPTH):
        mod = c_act @ ada_w[i] + ada_b[i]
        sh1, sc1, g1, sh2, sc2, g2 = [m[:, None, :] for m in jnp.split(mod, N_MOD, axis=-1)]
        h = _rmsnorm(x, norm1_g[i]) * (1.0 + sc1) + sh1
        j = i // N_MIXERS
        if i % N_MIXERS == 0:
            mix = _retention_mixer(h, ret_w_in[j], ret_log_decay[j], ret_gn_g[j], ret_w_out[j])
        else:
            mix = _gmlp_mixer(h, gm_w_in[j], gm_b_in[j], gm_vn_g[j], gm_vn_b[j],
                              gm_w_s[j], gm_b_s[j], gm_w_out[j], gm_b_out[j])
        x = x + g1 * mix
        h = _rmsnorm(x, norm2_g[i]) * (1.0 + sc2) + sh2
        y = _moe(h.reshape(B * S, D), moe_w_r[i], moe_b_r[i], moe_w_gu[i], moe_b_gu[i],
                 moe_w_dn[i], moe_b_dn[i]).reshape(B, S, D)
        x = x + g2 * y
    return _rmsnorm(x, final_g)


def setup_inputs(seed: int = 0) -> dict:
    key = jax.random.key(seed)
    ks = jax.random.split(key, 32)
    f32 = jnp.float32

    def nrm(k, shape, scale):
        return jax.random.normal(k, shape, f32) * scale

    D = D_MODEL
    return {
        "x_prompt": nrm(ks[0], (BATCH, SEQ, D), 1.0),
        "x_sample": nrm(ks[1], (DEC_BATCH, DEC_SEQ, D), 1.0),
        "c_prompt": nrm(ks[2], (BATCH, D), 1.0),
        "c_sample": nrm(ks[3], (DEC_BATCH, D), 1.0),
        "ada_w": nrm(ks[4], (DEPTH, D, N_MOD * D), 0.5 * D ** -0.5),
        "ada_b": nrm(ks[5], (DEPTH, N_MOD * D), 0.02),
        "norm1_g": 1.0 + nrm(ks[6], (DEPTH, D), 0.02),
        "norm2_g": 1.0 + nrm(ks[7], (DEPTH, D), 0.02),
        "ret_w_in": nrm(ks[8], (N_RET_LAYERS, D, RET_IN), D ** -0.5),
        "ret_log_decay": jnp.log1p(-(2.0 ** (-5.0 - jnp.arange(RET_HEADS, dtype=f32)
                                             + nrm(ks[9], (N_RET_LAYERS, 2, RET_HEADS), 0.1)))),
        "ret_gn_g": 1.0 + nrm(ks[10], (N_RET_LAYERS, RET_VDIM), 0.02),
        "ret_w_out": nrm(ks[11], (N_RET_LAYERS, RET_VDIM, D), RET_VDIM ** -0.5),
        "gm_w_in": nrm(ks[12], (N_GM_LAYERS, D, 2 * GM_WIDTH), D ** -0.5),
        "gm_b_in": nrm(ks[13], (N_GM_LAYERS, 2 * GM_WIDTH), 0.02),
        "gm_vn_g": 1.0 + nrm(ks[14], (N_GM_LAYERS, GM_WIDTH), 0.02),
        "gm_vn_b": nrm(ks[15], (N_GM_LAYERS, GM_WIDTH), 0.02),
        "gm_w_s": nrm(ks[16], (N_GM_LAYERS, GM_GROUPS, GM_CHUNK, GM_CHUNK), GM_CHUNK ** -0.5),
        "gm_b_s": 1.0 + nrm(ks[17], (N_GM_LAYERS, GM_GROUPS, GM_CHUNK), 0.02),
        "gm_w_out": nrm(ks[18], (N_GM_LAYERS, GM_WIDTH, D), GM_WIDTH ** -0.5),
        "gm_b_out": nrm(ks[19], (N_GM_LAYERS, D), 0.02),
        "moe_w_r": nrm(ks[20], (DEPTH, D, N_EXPERTS), D ** -0.5),
        "moe_b_r": nrm(ks[21], (DEPTH, N_EXPERTS), 0.01),
        "moe_w_gu": nrm(ks[22], (DEPTH, N_EXPERTS, D, 2 * D_EXPERT), D ** -0.5),
        "moe_b_gu": nrm(ks[23], (DEPTH, N_EXPERTS, 2 * D_EXPERT), 0.02),
        "moe_w_dn": nrm(ks[24], (DEPTH, N_EXPERTS, D_EXPERT, D), D_EXPERT ** -0.5),
        "moe_b_dn": nrm(ks[25], (DEPTH, N_EXPERTS, D), 0.02),
        "final_g": 1.0 + nrm(ks[26], (D,), 0.02),
    }


def reference(x_prompt, x_sample, c_prompt, c_sample, ada_w, ada_b, norm1_g, norm2_g,
              ret_w_in, ret_log_decay, ret_gn_g, ret_w_out,
              gm_w_in, gm_b_in, gm_vn_g, gm_vn_b, gm_w_s, gm_b_s, gm_w_out, gm_b_out,
              moe_w_r, moe_b_r, moe_w_gu, moe_b_gu, moe_w_dn, moe_b_dn, final_g):
    y_prompt = _trunk(x_prompt, c_prompt, ada_w, ada_b, norm1_g, norm2_g,
                      ret_w_in, ret_log_decay, ret_gn_g, ret_w_out,
                      gm_w_in, gm_b_in, gm_vn_g, gm_vn_b, gm_w_s, gm_b_s, gm_w_out, gm_b_out,
                      moe_w_r, moe_b_r, moe_w_gu, moe_b_gu, moe_w_dn, moe_b_dn, final_g)
    y_sample = _trunk(x_sample, c_sample, ada_w, ada_b, norm1_g, norm2_g,
                      ret_w_in, ret_log_decay, ret_gn_g, ret_w_out,
                      gm_w_in, gm_b_in, gm_vn_g, gm_vn_b, gm_w_s, gm_b_s, gm_w_out, gm_b_out,
                      moe_w_r, moe_b_r, moe_w_gu, moe_b_gu, moe_w_dn, moe_b_dn, final_g)
    return (y_prompt, y_sample)
```

```python
import functools

import jax
import jax.numpy as jnp
from jax import lax
from jax.experimental import pallas as pl
from jax.experimental.pallas import tpu as pltpu

F32 = jnp.float32
BF16 = jnp.bfloat16
HIGHEST = lax.Precision.HIGHEST

D_MODEL = 1024
DEPTH = 2
N_MOD = 6
EPS = 1e-6
RET_HEADS = 4
RET_DK = D_MODEL // RET_HEADS
RET_DV = 2 * RET_DK
RET_QK = RET_HEADS * RET_DK
RET_VDIM = RET_HEADS * RET_DV
RET_IN = 2 * RET_QK + 2 * RET_VDIM
ROPE_BASE = 10000.0
ROPE_HALF = RET_DK // 2
RET_BLOCK = 256
GM_GROUPS = 4
GM_WIDTH = 2 * D_MODEL
GM_GDIM = GM_WIDTH // GM_GROUPS
GM_CHUNK = 128
N_EXPERTS = 32
TOP_K = 4
D_EXPERT = D_MODEL
SWIGLU_LIMIT = 7.0
SWIGLU_ALPHA = 1.702
MOE_BLOCK = 256
LANES = 128

VMEM_LIMIT = 56 * 1024 * 1024


def _cparams(sem):
    return pltpu.CompilerParams(dimension_semantics=sem, vmem_limit_bytes=VMEM_LIMIT)


def _norm_mod(x, g, sc, sh):
    y = x * lax.rsqrt(jnp.mean(x * x, axis=-1, keepdims=True) + EPS) * g
    return y * (1.0 + sc) + sh


def _ada_kernel(c_ref, w_ref, b_ref, o_ref):
    ca = jax.nn.silu(c_ref[...])
    o_ref[0] = jnp.dot(ca, w_ref[0], preferred_element_type=F32, precision=HIGHEST) + b_ref[0]


def _ada(c_all, ada_w, ada_b):
    r = c_all.shape[0]
    tn = 768
    n = N_MOD * D_MODEL
    return pl.pallas_call(
        _ada_kernel,
        grid=(DEPTH, n // tn),
        in_specs=[
            pl.BlockSpec((r, D_MODEL), lambda l, j: (0, 0)),
            pl.BlockSpec((1, D_MODEL, tn), lambda l, j: (l, 0, j)),
            pl.BlockSpec((1, 1, tn), lambda l, j: (l, 0, j)),
        ],
        out_specs=pl.BlockSpec((1, r, tn), lambda l, j: (l, 0, j)),
        out_shape=jax.ShapeDtypeStruct((DEPTH, r, n), F32),
        compiler_params=_cparams(("arbitrary", "arbitrary")),
        name="ada_mod",
    )(c_all, ada_w, ada_b.reshape(DEPTH, 1, n))


def _inproj_ret_kernel(x_ref, sc_ref, sh_ref, g_ref, w_ref, cos_ref, sin_ref, o_ref, h_scr, *, tn):
    j = pl.program_id(1)
    n_q = RET_QK // tn
    n_rope = 2 * RET_QK // tn

    @pl.when(j == 0)
    def _():
        h_scr[...] = _norm_mod(x_ref[...], g_ref[...], sc_ref[0], sh_ref[0]).astype(BF16)

    acc = jnp.dot(h_scr[...], w_ref[...], preferred_element_type=F32)

    @pl.when(j < n_rope)
    def _():
        scale = jnp.where(j < n_q, RET_DK ** -0.5, 1.0).astype(F32)
        cos = cos_ref[...]
        sin = sin_ref[...]
        for hh in range(tn // RET_DK):
            lo = hh * RET_DK
            x1 = acc[:, lo:lo + ROPE_HALF]
            x2 = acc[:, lo + ROPE_HALF:lo + RET_DK]
            o_ref[:, lo:lo + ROPE_HALF] = ((x1 * cos - x2 * sin) * scale).astype(BF16)
            o_ref[:, lo + ROPE_HALF:lo + RET_DK] = ((x2 * cos + x1 * sin) * scale).astype(BF16)

    @pl.when(j >= n_rope)
    def _():
        o_ref[...] = acc.astype(BF16)


def _inproj_ret(x, sc, sh, g, w_bf, cos, sin, seq):
    t = x.shape[0]
    tm, tn = 512, 512
    spb = seq // tm
    return pl.pallas_call(
        functools.partial(_inproj_ret_kernel, tn=tn),
        grid=(t // tm, RET_IN // tn),
        in_specs=[
            pl.BlockSpec((tm, D_MODEL), lambda i, j: (i, 0)),
            pl.BlockSpec((1, 1, D_MODEL), lambda i, j: (i // spb, 0, 0)),
            pl.BlockSpec((1, 1, D_MODEL), lambda i, j: (i // spb, 0, 0)),
            pl.BlockSpec((1, D_MODEL), lambda i, j: (0, 0)),
            pl.BlockSpec((D_MODEL, tn), lambda i, j: (0, j)),
            pl.BlockSpec((tm, ROPE_HALF), lambda i, j: (i % spb, 0)),
            pl.BlockSpec((tm, ROPE_HALF), lambda i, j: (i % spb, 0)),
        ],
        out_specs=pl.BlockSpec((tm, tn), lambda i, j: (i, j)),
        out_shape=jax.ShapeDtypeStruct((t, RET_IN), BF16),
        scratch_shapes=[pltpu.VMEM((tm, D_MODEL), BF16)],
        compiler_params=_cparams(("arbitrary", "arbitrary")),
        name="ret_inproj",
    )(x, sc, sh, g, w_bf, cos, sin)


def _ret_tables(ld_ref, first, d_scr, qd_scr, kd_scr, s_scr, *, backward):
    c = RET_BLOCK

    @pl.when(first)
    def _():
        r = lax.broadcasted_iota(jnp.int32, (c, RET_DK), 0).astype(F32)
        if d_scr is not None:
            ri = lax.broadcasted_iota(jnp.int32, (c, c), 0).astype(F32)
            ci = lax.broadcasted_iota(jnp.int32, (c, c), 1).astype(F32)
            diff = ri - ci
        for h in range(RET_HEADS):
            lf = ld_ref[h]
            lb = ld_ref[RET_HEADS + h]
            if d_scr is not None:
                d_scr[h] = jnp.where(diff >= 0.0, jnp.exp(lf * jnp.maximum(diff, 0.0)),
                                     jnp.exp(lb * jnp.maximum(-diff, 0.0)))
            if backward:
                qd_scr[h] = jnp.exp(lb * (float(c) - r))
                kd_scr[h] = jnp.exp(lb * r)
            else:
                qd_scr[h] = jnp.exp(lf * (r + 1.0))
                kd_scr[h] = jnp.exp(lf * (float(c) - 1.0 - r))
            s_scr[h] = jnp.zeros((RET_DK, RET_DV), F32)


def _ret_head_step(h, ld, q_ref, k_ref, v_ref, qd_scr, kd_scr, s_scr):
    c = RET_BLOCK
    q = q_ref[:, h * RET_DK:(h + 1) * RET_DK]
    k = k_ref[:, h * RET_DK:(h + 1) * RET_DK]
    v = v_ref[:, h * RET_DV:(h + 1) * RET_DV]
    s = s_scr[h]
    qs = (q.astype(F32) * qd_scr[h]).astype(BF16)
    cross = jnp.dot(qs, s.astype(BF16), preferred_element_type=F32)
    ks = (k.astype(F32) * kd_scr[h]).astype(BF16)
    cd = jnp.exp(jnp.full((1, RET_DV), ld * float(c), F32))
    s_scr[h] = s * cd + lax.dot_general(ks, v, (((0,), (0,)), ((), ())), preferred_element_type=F32)
    return q, k, v, cross


def _ret_fwd_kernel(ld_ref, q_ref, k_ref, v_ref, o_ref, d_scr, qd_scr, kd_scr, s_scr):
    _ret_tables(ld_ref, pl.program_id(1) == 0, d_scr, qd_scr, kd_scr, s_scr, backward=False)
    for h in range(RET_HEADS):
        q, k, v, cross = _ret_head_step(h, ld_ref[h], q_ref, k_ref, v_ref, qd_scr, kd_scr, s_scr)
        s = lax.dot_general(q, k, (((1,), (1,)), ((), ())), preferred_element_type=F32) * d_scr[h]
        o_ref[:, h * RET_DV:(h + 1) * RET_DV] = jnp.dot(s.astype(BF16), v, preferred_element_type=F32) + cross


def _ret_fwd(proj, ld, batch, seq):
    c = RET_BLOCK
    nc = seq // c
    t = batch * seq
    return pl.pallas_call(
        _ret_fwd_kernel,
        grid=(batch, nc),
        in_specs=[
            pl.BlockSpec(memory_space=pltpu.SMEM),
            pl.BlockSpec((c, RET_QK), lambda b, i: (b * nc + i, 0)),
            pl.BlockSpec((c, RET_QK), lambda b, i: (b * nc + i, 1)),
            pl.BlockSpec((c, RET_VDIM), lambda b, i: (b * nc + i, 1)),
        ],
        out_specs=pl.BlockSpec((c, RET_VDIM), lambda b, i: (b * nc + i, 0)),
        out_shape=jax.ShapeDtypeStruct((t, RET_VDIM), F32),
        scratch_shapes=[
            pltpu.VMEM((RET_HEADS, c, c), F32),
            pltpu.VMEM((RET_HEADS, c, RET_DK), F32),
            pltpu.VMEM((RET_HEADS, c, RET_DK), F32),
            pltpu.VMEM((RET_HEADS, RET_DK, RET_DV), F32),
        ],
        compiler_params=_cparams(("arbitrary", "arbitrary")),
        name="ret_fwd_scan",
    )(ld, proj, proj, proj)


def _ret_bwd_kernel(ld_ref, q_ref, k_ref, v_ref, gt_ref, o1_ref, x_ref, g1_ref, gn_ref, wo_ref, out_ref,
                    qd_scr, kd_scr, s_scr, og_scr):
    _ret_tables(ld_ref, pl.program_id(1) == 0, None, qd_scr, kd_scr, s_scr, backward=True)
    for h in range(RET_HEADS):
        _, _, _, cross = _ret_head_step(h, ld_ref[RET_HEADS + h], q_ref, k_ref, v_ref, qd_scr, kd_scr, s_scr)
        sl = slice(h * RET_DV, (h + 1) * RET_DV)
        o = o1_ref[:, sl] + cross
        mu = jnp.mean(o, axis=-1, keepdims=True)
        d = o - mu
        var = jnp.mean(d * d, axis=-1, keepdims=True)
        on = d * lax.rsqrt(var + EPS) * gn_ref[:, sl]
        og_scr[:, sl] = (on * jax.nn.silu(gt_ref[:, sl].astype(F32))).astype(BF16)
    mix = jnp.dot(og_scr[...], wo_ref[...], preferred_element_type=F32)
    out_ref[...] = x_ref[...] + g1_ref[0] * mix


def _ret_bwd(proj, o1, x, g1, gn_g, wo_bf, ld, batch, seq):
    c = RET_BLOCK
    nc = seq // c
    t = batch * seq
    rev = lambda b, i: b * nc + (nc - 1 - i)
    return pl.pallas_call(
        _ret_bwd_kernel,
        grid=(batch, nc),
        in_specs=[
            pl.BlockSpec(memory_space=pltpu.SMEM),
            pl.BlockSpec((c, RET_QK), lambda b, i: (rev(b, i), 0)),
            pl.BlockSpec((c, RET_QK), lambda b, i: (rev(b, i), 1)),
            pl.BlockSpec((c, RET_VDIM), lambda b, i: (rev(b, i), 1)),
            pl.BlockSpec((c, RET_VDIM), lambda b, i: (rev(b, i), 2)),
            pl.BlockSpec((c, RET_VDIM), lambda b, i: (rev(b, i), 0)),
            pl.BlockSpec((c, D_MODEL), lambda b, i: (rev(b, i), 0)),
            pl.BlockSpec((1, 1, D_MODEL), lambda b, i: (b, 0, 0)),
            pl.BlockSpec((1, RET_VDIM), lambda b, i: (0, 0)),
            pl.BlockSpec((RET_VDIM, D_MODEL), lambda b, i: (0, 0)),
        ],
        out_specs=pl.BlockSpec((c, D_MODEL), lambda b, i: (rev(b, i), 0)),
        out_shape=jax.ShapeDtypeStruct((t, D_MODEL), F32),
        scratch_shapes=[
            pltpu.VMEM((RET_HEADS, c, RET_DK), F32),
            pltpu.VMEM((RET_HEADS, c, RET_DK), F32),
            pltpu.VMEM((RET_HEADS, RET_DK, RET_DV), F32),
            pltpu.VMEM((c, RET_VDIM), BF16),
        ],
        compiler_params=_cparams(("arbitrary", "arbitrary")),
        name="ret_bwd_scan_out",
    )(ld, proj, proj, proj, proj, o1, x, g1, gn_g, wo_bf)


def _gmlp_kernel(x_ref, sc_ref, sh_ref, g1_ref, ng_ref, wi_ref, bi_ref, vg_ref, vb_ref, ws_ref, bs_ref,
                 wo_ref, bo_ref, out_ref, gated_scr, *, tm):
    x = x_ref[...]
    h = _norm_mod(x, ng_ref[...], sc_ref[0], sh_ref[0]).astype(BF16)
    z = jnp.dot(h, wi_ref[...], preferred_element_type=F32) + bi_ref[...]
    z = 0.5 * z * (1.0 + lax.erf(z * (2.0 ** -0.5)))
    u = z[:, :GM_WIDTH]
    v = z[:, GM_WIDTH:]
    mu = jnp.mean(v, axis=-1, keepdims=True)
    d = v - mu
    var = jnp.mean(d * d, axis=-1, keepdims=True)
    vn = (d * lax.rsqrt(var + EPS) * vg_ref[...] + vb_ref[...]).astype(BF16)
    for n in range(tm // GM_CHUNK):
        rows = slice(n * GM_CHUNK, (n + 1) * GM_CHUNK)
        for g in range(GM_GROUPS):
            cols = slice(g * GM_GDIM, (g + 1) * GM_GDIM)
            mixed = jnp.dot(ws_ref[g], vn[rows, cols], preferred_element_type=F32) + bs_ref[g]
            gated_scr[rows, cols] = (u[rows, cols] * mixed).astype(BF16)
    mix = jnp.dot(gated_scr[...], wo_ref[...], preferred_element_type=F32) + bo_ref[...]
    out_ref[...] = x + g1_ref[0] * mix


def _gmlp(x, sc, sh, g1, ng, wi_bf, bi, vg, vb, ws_bf, bs_col, wo_bf, bo, seq):
    t = x.shape[0]
    tm = 256
    spb = seq // tm
    const2 = lambda i: (0, 0)
    const3 = lambda i: (0, 0, 0)
    per_b = lambda i: (i // spb, 0, 0)
    return pl.pallas_call(
        functools.partial(_gmlp_kernel, tm=tm),
        grid=(t // tm,),
        in_specs=[
            pl.BlockSpec((tm, D_MODEL), lambda i: (i, 0)),
            pl.BlockSpec((1, 1, D_MODEL), per_b),
            pl.BlockSpec((1, 1, D_MODEL), per_b),
            pl.BlockSpec((1, 1, D_MODEL), per_b),
            pl.BlockSpec((1, D_MODEL), const2),
            pl.BlockSpec((D_MODEL, 2 * GM_WIDTH), const2),
            pl.BlockSpec((1, 2 * GM_WIDTH), const2),
            pl.BlockSpec((1, GM_WIDTH), const2),
            pl.BlockSpec((1, GM_WIDTH), const2),
            pl.BlockSpec((GM_GROUPS, GM_CHUNK, GM_CHUNK), const3),
            pl.BlockSpec((GM_GROUPS, GM_CHUNK, 1), const3),
            pl.BlockSpec((GM_WIDTH, D_MODEL), const2),
            pl.BlockSpec((1, D_MODEL), const2),
        ],
        out_specs=pl.BlockSpec((tm, D_MODEL), lambda i: (i, 0)),
        out_shape=jax.ShapeDtypeStruct((t, D_MODEL), F32),
        scratch_shapes=[pltpu.VMEM((tm, GM_WIDTH), BF16)],
        compiler_params=_cparams(("arbitrary",)),
        name="gmlp_mixer",
    )(x, sc, sh, g1, ng, wi_bf, bi, vg, vb, ws_bf, bs_col, wo_bf, bo)


def _router_kernel(x_ref, sc_ref, sh_ref, ng_ref, wr_ref, br_ref, h_ref, ti_ref, tg_ref, *, tm):
    h = _norm_mod(x_ref[...], ng_ref[...], sc_ref[0], sh_ref[0])
    h_ref[...] = h.astype(BF16)
    logits = jnp.dot(h, wr_ref[...], preferred_element_type=F32, precision=HIGHEST) + br_ref[...]
    lane_e = lax.broadcasted_iota(jnp.int32, (tm, N_EXPERTS), 1)
    lane = lax.broadcasted_iota(jnp.int32, (tm, LANES), 1)
    ti = jnp.zeros((tm, LANES), jnp.int32)
    tv = jnp.zeros((tm, LANES), F32)
    l = logits
    m0 = None
    for k in range(TOP_K):
        m = jnp.max(l, axis=-1, keepdims=True)
        idx = jnp.min(jnp.where(l == m, lane_e, N_EXPERTS), axis=-1, keepdims=True)
        l = jnp.where(lane_e == idx, -jnp.inf, l)
        if k == 0:
            m0 = m
        ti = jnp.where(lane == k, idx, ti)
        tv = jnp.where(lane == k, jnp.exp(m - m0), tv)
    ti_ref[...] = ti
    tg_ref[...] = tv / jnp.sum(tv, axis=-1, keepdims=True)


def _router(x, sc, sh, ng, w_r, b_r, seq):
    t = x.shape[0]
    tm = 512
    spb = seq // tm
    per_b = lambda i: (i // spb, 0, 0)
    return pl.pallas_call(
        functools.partial(_router_kernel, tm=tm),
        grid=(t // tm,),
        in_specs=[
            pl.BlockSpec((tm, D_MODEL), lambda i: (i, 0)),
            pl.BlockSpec((1, 1, D_MODEL), per_b),
            pl.BlockSpec((1, 1, D_MODEL), per_b),
            pl.BlockSpec((1, D_MODEL), lambda i: (0, 0)),
            pl.BlockSpec((D_MODEL, N_EXPERTS), lambda i: (0, 0)),
            pl.BlockSpec((1, N_EXPERTS), lambda i: (0, 0)),
        ],
        out_specs=[
            pl.BlockSpec((tm, D_MODEL), lambda i: (i, 0)),
            pl.BlockSpec((tm, LANES), lambda i: (i, 0)),
            pl.BlockSpec((tm, LANES), lambda i: (i, 0)),
        ],
        out_shape=[
            jax.ShapeDtypeStruct((t, D_MODEL), BF16),
            jax.ShapeDtypeStruct((t, LANES), jnp.int32),
            jax.ShapeDtypeStruct((t, LANES), F32),
        ],
        compiler_params=_cparams(("arbitrary",)),
        name="moe_norm_router",
    )(x, sc, sh, ng, w_r, b_r)


def _expert_kernel(be_ref, x_ref, wgu_ref, bgu_ref, wdn_ref, bdn_ref, y_ref):
    del be_ref
    hg = jnp.dot(x_ref[...], wgu_ref[0], preferred_element_type=F32) + bgu_ref[0]
    glu = jnp.minimum(hg[:, :D_EXPERT], SWIGLU_LIMIT)
    lin = jnp.clip(hg[:, D_EXPERT:], -SWIGLU_LIMIT, SWIGLU_LIMIT)
    act = glu * jax.nn.sigmoid(SWIGLU_ALPHA * glu) * (lin + 1.0)
    y = jnp.dot(act.astype(BF16), wdn_ref[0], preferred_element_type=F32) + bdn_ref[0]
    y_ref[...] = y.astype(y_ref.dtype)


def _experts(xb, block_e, wgu_bf, bgu, wdn_bf, bdn):
    p = xb.shape[0]
    nb = p // MOE_BLOCK
    grid_spec = pltpu.PrefetchScalarGridSpec(
        num_scalar_prefetch=1,
        grid=(nb,),
        in_specs=[
            pl.BlockSpec((MOE_BLOCK, D_MODEL), lambda i, be: (i, 0)),
            pl.BlockSpec((1, D_MODEL, 2 * D_EXPERT), lambda i, be: (be[i], 0, 0)),
            pl.BlockSpec((1, 1, 2 * D_EXPERT), lambda i, be: (be[i], 0, 0)),
            pl.BlockSpec((1, D_EXPERT, D_MODEL), lambda i, be: (be[i], 0, 0)),
            pl.BlockSpec((1, 1, D_MODEL), lambda i, be: (be[i], 0, 0)),
        ],
        out_specs=pl.BlockSpec((MOE_BLOCK, D_MODEL), lambda i, be: (i, 0)),
    )
    return pl.pallas_call(
        _expert_kernel,
        grid_spec=grid_spec,
        out_shape=jax.ShapeDtypeStruct((p, D_MODEL), BF16),
        compiler_params=_cparams(("arbitrary",)),
        name="moe_experts",
    )(block_e, xb, wgu_bf, bgu, wdn_bf, bdn)


def _combine_kernel(yk_ref, tg_ref, x_ref, g2_ref, fg_ref, out_ref, *, final):
    tg = tg_ref[...]
    y = tg[:, 0:1] * yk_ref[0].astype(F32)
    for k in range(1, TOP_K):
        y = y + tg[:, k:k + 1] * yk_ref[k].astype(F32)
    xn = x_ref[...] + g2_ref[0] * y
    if final:
        xn = xn * lax.rsqrt(jnp.mean(xn * xn, axis=-1, keepdims=True) + EPS) * fg_ref[...]
    out_ref[...] = xn


def _combine(yk, tg, x, g2, final_g, seq, final):
    t = x.shape[0]
    tm = 512
    spb = seq // tm
    return pl.pallas_call(
        functools.partial(_combine_kernel, final=final),
        grid=(t // tm,),
        in_specs=[
            pl.BlockSpec((TOP_K, tm, D_MODEL), lambda i: (0, i, 0)),
            pl.BlockSpec((tm, LANES), lambda i: (i, 0)),
            pl.BlockSpec((tm, D_MODEL), lambda i: (i, 0)),
            pl.BlockSpec((1, 1, D_MODEL), lambda i: (i // spb, 0, 0)),
            pl.BlockSpec((1, D_MODEL), lambda i: (0, 0)),
        ],
        out_specs=pl.BlockSpec((tm, D_MODEL), lambda i: (i, 0)),
        out_shape=jax.ShapeDtypeStruct((t, D_MODEL), F32),
        compiler_params=_cparams(("arbitrary",)),
        name="moe_combine",
    )(yk, tg, x, g2, final_g)


def _moe_layer(x, sc2, sh2, g2, ng, w_r, b_r, wgu_bf, bgu, wdn_bf, bdn, final_g, seq, final):
    t = x.shape[0]
    h, ti, tg = _router(x, sc2, sh2, ng, w_r, b_r, seq)
    top_i = ti[:, :TOP_K]
    a = t * TOP_K
    flat_e = top_i.reshape(a)
    order = jnp.argsort(flat_e)
    se = flat_e[order]
    counts = jnp.bincount(flat_e, length=N_EXPERTS)
    padded = (counts + MOE_BLOCK - 1) // MOE_BLOCK * MOE_BLOCK
    start = jnp.cumsum(counts) - counts
    pend = jnp.cumsum(padded)
    pstart = pend - padded
    dest = (pstart[se] + (jnp.arange(a, dtype=jnp.int32) - start[se])).astype(jnp.int32)
    n_blocks = -(-a // MOE_BLOCK) + N_EXPERTS
    p = n_blocks * MOE_BLOCK
    row_tok = jnp.zeros((p,), jnp.int32).at[dest].set((order // TOP_K).astype(jnp.int32))
    pos = jnp.zeros((a,), jnp.int32).at[order].set(dest).reshape(t, TOP_K)
    block_e = jnp.minimum(
        jnp.searchsorted(pend, jnp.arange(n_blocks, dtype=jnp.int32) * MOE_BLOCK, side='right'),
        N_EXPERTS - 1).astype(jnp.int32)
    xb = h[row_tok]
    yb = _experts(xb, block_e, wgu_bf, bgu, wdn_bf, bdn)
    yk = yb[pos.T]
    return _combine(yk, tg, x, g2, final_g, seq, final)


def _rope_tables(seq):
    inv = ROPE_BASE ** (-jnp.arange(ROPE_HALF, dtype=F32) / ROPE_HALF)
    ang = jnp.arange(seq, dtype=F32)[:, None] * inv[None, :]
    return jnp.cos(ang), jnp.sin(ang)


def _trunk(x3, mod, wts):
    batch, seq, _ = x3.shape
    x = x3.reshape(batch * seq, D_MODEL)
    cos, sin = _rope_tables(seq)
    for i in range(DEPTH):
        sh1, sc1, g1, sh2, sc2, g2 = [m.reshape(batch, 1, D_MODEL) for m in jnp.split(mod[i], N_MOD, axis=-1)]
        if i == 0:
            proj = _inproj_ret(x, sc1, sh1, wts["norm1_g"][i], wts["ret_w_in"], cos, sin, seq)
            o1 = _ret_fwd(proj, wts["ret_ld"], batch, seq)
            x = _ret_bwd(proj, o1, x, g1, wts["ret_gn_g"], wts["ret_w_out"], wts["ret_ld"], batch, seq)
        else:
            x = _gmlp(x, sc1, sh1, g1, wts["norm1_g"][i], wts["gm_w_in"], wts["gm_b_in"], wts["gm_vn_g"],
                      wts["gm_vn_b"], wts["gm_w_s"], wts["gm_b_s"], wts["gm_w_out"], wts["gm_b_out"], seq)
        x = _moe_layer(x, sc2, sh2, g2, wts["norm2_g"][i], wts["moe_w_r"][i], wts["moe_b_r"][i],
                       wts["moe_w_gu"][i], wts["moe_b_gu"][i], wts["moe_w_dn"][i], wts["moe_b_dn"][i],
                       wts["final_g"], seq, final=(i == DEPTH - 1))
    return x.reshape(batch, seq, D_MODEL)


def kernel(x_prompt, x_sample, c_prompt, c_sample, ada_w, ada_b, norm1_g, norm2_g, ret_w_in, ret_log_decay,
           ret_gn_g, ret_w_out, gm_w_in, gm_b_in, gm_vn_g, gm_vn_b, gm_w_s, gm_b_s, gm_w_out, gm_b_out,
           moe_w_r, moe_b_r, moe_w_gu, moe_b_gu, moe_w_dn, moe_b_dn, final_g):
    bp = c_prompt.shape[0]
    bs = c_sample.shape[0]
    mod = _ada(jnp.concatenate([c_prompt, c_sample], axis=0), ada_w, ada_b)
    wgu = jnp.concatenate([moe_w_gu[..., 0::2], moe_w_gu[..., 1::2]], axis=-1).astype(BF16)
    bgu = jnp.concatenate([moe_b_gu[..., 0::2], moe_b_gu[..., 1::2]], axis=-1)
    wts = dict(
        norm1_g=norm1_g.reshape(DEPTH, 1, D_MODEL),
        norm2_g=norm2_g.reshape(DEPTH, 1, D_MODEL),
        ret_w_in=ret_w_in[0].astype(BF16),
        ret_ld=ret_log_decay[0].reshape(2 * RET_HEADS).astype(F32),
        ret_gn_g=ret_gn_g[0].reshape(1, RET_VDIM),
        ret_w_out=ret_w_out[0].astype(BF16),
        gm_w_in=gm_w_in[0].astype(BF16),
        gm_b_in=gm_b_in[0].reshape(1, 2 * GM_WIDTH),
        gm_vn_g=gm_vn_g[0].reshape(1, GM_WIDTH),
        gm_vn_b=gm_vn_b[0].reshape(1, GM_WIDTH),
        gm_w_s=gm_w_s[0].astype(BF16),
        gm_b_s=gm_b_s[0].reshape(GM_GROUPS, GM_CHUNK, 1),
        gm_w_out=gm_w_out[0].astype(BF16),
        gm_b_out=gm_b_out[0].reshape(1, D_MODEL),
        moe_w_r=moe_w_r,
        moe_b_r=moe_b_r.reshape(DEPTH, 1, N_EXPERTS),
        moe_w_gu=wgu,
        moe_b_gu=bgu.reshape(DEPTH, N_EXPERTS, 1, 2 * D_EXPERT),
        moe_w_dn=moe_w_dn.astype(BF16),
        moe_b_dn=moe_b_dn.reshape(DEPTH, N_EXPERTS, 1, D_MODEL),
        final_g=final_g.reshape(1, D_MODEL),
    )
    y_prompt = _trunk(x_prompt, mod[:, :bp], wts)
    y_sample = _trunk(x_sample, mod[:, bp:bp + bs], wts)
    return (y_prompt, y_sample)
```

```python
import functools

import jax
import jax.numpy as jnp
from jax import lax
from jax.experimental import pallas as pl
from jax.experimental.pallas import tpu as pltpu

F32 = jnp.float32
BF16 = jnp.bfloat16
HIGHEST = lax.Precision.HIGHEST

D_MODEL = 1024
DEPTH = 2
N_MOD = 6
EPS = 1e-6
RET_HEADS = 4
RET_DK = D_MODEL // RET_HEADS
RET_DV = 2 * RET_DK
RET_QK = RET_HEADS * RET_DK
RET_VDIM = RET_HEADS * RET_DV
RET_IN = 2 * RET_QK + 2 * RET_VDIM
ROPE_BASE = 10000.0
ROPE_HALF = RET_DK // 2
RET_BLOCK = 256
GM_GROUPS = 4
GM_WIDTH = 2 * D_MODEL
GM_GDIM = GM_WIDTH // GM_GROUPS
GM_CHUNK = 128
N_EXPERTS = 32
TOP_K = 4
D_EXPERT = D_MODEL
SWIGLU_LIMIT = 7.0
SWIGLU_ALPHA = 1.702
MOE_BLOCK = 256
LANES = 128
GU_GROUP = 2 * LANES

VMEM_LIMIT = 56 * 1024 * 1024


def _cparams(sem):
    return pltpu.CompilerParams(dimension_semantics=sem, vmem_limit_bytes=VMEM_LIMIT)


def _norm_mod(x, g, sc, sh):
    y = x * lax.rsqrt(jnp.mean(x * x, axis=-1, keepdims=True) + EPS) * g
    return y * (1.0 + sc) + sh


def _ada_kernel(c_ref, w_ref, b_ref, o_ref):
    ca = jax.nn.silu(c_ref[...])
    o_ref[0] = jnp.dot(ca, w_ref[0], preferred_element_type=F32, precision=HIGHEST) + b_ref[0]


def _ada(c_all, ada_w, ada_b):
    r = c_all.shape[0]
    tn = 768
    n = N_MOD * D_MODEL
    return pl.pallas_call(
        _ada_kernel,
        grid=(DEPTH, n // tn),
        in_specs=[
            pl.BlockSpec((r, D_MODEL), lambda l, j: (0, 0)),
            pl.BlockSpec((1, D_MODEL, tn), lambda l, j: (l, 0, j)),
            pl.BlockSpec((1, 1, tn), lambda l, j: (l, 0, j)),
        ],
        out_specs=pl.BlockSpec((1, r, tn), lambda l, j: (l, 0, j)),
        out_shape=jax.ShapeDtypeStruct((DEPTH, r, n), F32),
        compiler_params=_cparams(("arbitrary", "arbitrary")),
        name="ada_mod",
    )(c_all, ada_w, ada_b.reshape(DEPTH, 1, n))


def _inproj_ret_kernel(x_ref, sc_ref, sh_ref, g_ref, w_ref, cos_ref, sin_ref, o_ref, h_scr, *, tn):
    j = pl.program_id(1)
    n_q = RET_QK // tn
    n_rope = 2 * RET_QK // tn

    @pl.when(j == 0)
    def _():
        h_scr[...] = _norm_mod(x_ref[...], g_ref[...], sc_ref[0], sh_ref[0]).astype(BF16)

    acc = jnp.dot(h_scr[...], w_ref[...], preferred_element_type=F32)

    @pl.when(j < n_rope)
    def _():
        scale = jnp.where(j < n_q, RET_DK ** -0.5, 1.0).astype(F32)
        cos = cos_ref[...]
        sin = sin_ref[...]
        for hh in range(tn // RET_DK):
            lo = hh * RET_DK
            x1 = acc[:, lo:lo + ROPE_HALF]
            x2 = acc[:, lo + ROPE_HALF:lo + RET_DK]
            o_ref[:, lo:lo + ROPE_HALF] = ((x1 * cos - x2 * sin) * scale).astype(BF16)
            o_ref[:, lo + ROPE_HALF:lo + RET_DK] = ((x2 * cos + x1 * sin) * scale).astype(BF16)

    @pl.when(j >= n_rope)
    def _():
        o_ref[...] = acc.astype(BF16)


def _inproj_ret(x, sc, sh, g, w_bf, cos, sin, seq):
    t = x.shape[0]
    tm, tn = 512, 512
    spb = seq // tm
    return pl.pallas_call(
        functools.partial(_inproj_ret_kernel, tn=tn),
        grid=(t // tm, RET_IN // tn),
        in_specs=[
            pl.BlockSpec((tm, D_MODEL), lambda i, j: (i, 0)),
            pl.BlockSpec((1, 1, D_MODEL), lambda i, j: (i // spb, 0, 0)),
            pl.BlockSpec((1, 1, D_MODEL), lambda i, j: (i // spb, 0, 0)),
            pl.BlockSpec((1, D_MODEL), lambda i, j: (0, 0)),
            pl.BlockSpec((D_MODEL, tn), lambda i, j: (0, j)),
            pl.BlockSpec((tm, ROPE_HALF), lambda i, j: (i % spb, 0)),
            pl.BlockSpec((tm, ROPE_HALF), lambda i, j: (i % spb, 0)),
        ],
        out_specs=pl.BlockSpec((tm, tn), lambda i, j: (i, j)),
        out_shape=jax.ShapeDtypeStruct((t, RET_IN), BF16),
        scratch_shapes=[pltpu.VMEM((tm, D_MODEL), BF16)],
        compiler_params=_cparams(("arbitrary", "arbitrary")),
        name="ret_inproj",
    )(x, sc, sh, g, w_bf, cos, sin)


def _ret_tables(ld_ref, first, d_scr, qd_scr, kd_scr, s_scr, *, backward):
    c = RET_BLOCK

    @pl.when(first)
    def _():
        r = lax.broadcasted_iota(jnp.int32, (c, RET_DK), 0).astype(F32)
        if d_scr is not None:
            ri = lax.broadcasted_iota(jnp.int32, (c, c), 0).astype(F32)
            ci = lax.broadcasted_iota(jnp.int32, (c, c), 1).astype(F32)
            diff = ri - ci
        for h in range(RET_HEADS):
            lf = ld_ref[h]
            lb = ld_ref[RET_HEADS + h]
            if d_scr is not None:
                d_scr[h] = jnp.where(diff >= 0.0, jnp.exp(lf * jnp.maximum(diff, 0.0)),
                                     jnp.exp(lb * jnp.maximum(-diff, 0.0)))
            if backward:
                qd_scr[h] = jnp.exp(lb * (float(c) - r))
                kd_scr[h] = jnp.exp(lb * r)
            else:
                qd_scr[h] = jnp.exp(lf * (r + 1.0))
                kd_scr[h] = jnp.exp(lf * (float(c) - 1.0 - r))
            s_scr[h] = jnp.zeros((RET_DK, RET_DV), F32)


def _ret_head_step(h, ld, q_ref, k_ref, v_ref, qd_scr, kd_scr, s_scr):
    c = RET_BLOCK
    q = q_ref[:, h * RET_DK:(h + 1) * RET_DK]
    k = k_ref[:, h * RET_DK:(h + 1) * RET_DK]
    v = v_ref[:, h * RET_DV:(h + 1) * RET_DV]
    s = s_scr[h]
    qs = (q.astype(F32) * qd_scr[h]).astype(BF16)
    cross = jnp.dot(qs, s.astype(BF16), preferred_element_type=F32)
    ks = (k.astype(F32) * kd_scr[h]).astype(BF16)
    cd = jnp.exp(jnp.full((1, RET_DV), ld * float(c), F32))
    s_scr[h] = s * cd + lax.dot_general(ks, v, (((0,), (0,)), ((), ())), preferred_element_type=F32)
    return q, k, v, cross


def _ret_fwd_kernel(ld_ref, q_ref, k_ref, v_ref, o_ref, d_scr, qd_scr, kd_scr, s_scr):
    _ret_tables(ld_ref, pl.program_id(1) == 0, d_scr, qd_scr, kd_scr, s_scr, backward=False)
    for h in range(RET_HEADS):
        q, k, v, cross = _ret_head_step(h, ld_ref[h], q_ref, k_ref, v_ref, qd_scr, kd_scr, s_scr)
        s = lax.dot_general(q, k, (((1,), (1,)), ((), ())), preferred_element_type=F32) * d_scr[h]
        o_ref[:, h * RET_DV:(h + 1) * RET_DV] = jnp.dot(s.astype(BF16), v, preferred_element_type=F32) + cross


def _ret_fwd(proj, ld, batch, seq):
    c = RET_BLOCK
    nc = seq // c
    t = batch * seq
    return pl.pallas_call(
        _ret_fwd_kernel,
        grid=(batch, nc),
        in_specs=[
            pl.BlockSpec(memory_space=pltpu.SMEM),
            pl.BlockSpec((c, RET_QK), lambda b, i: (b * nc + i, 0)),
            pl.BlockSpec((c, RET_QK), lambda b, i: (b * nc + i, 1)),
            pl.BlockSpec((c, RET_VDIM), lambda b, i: (b * nc + i, 1)),
        ],
        out_specs=pl.BlockSpec((c, RET_VDIM), lambda b, i: (b * nc + i, 0)),
        out_shape=jax.ShapeDtypeStruct((t, RET_VDIM), F32),
        scratch_shapes=[
            pltpu.VMEM((RET_HEADS, c, c), F32),
            pltpu.VMEM((RET_HEADS, c, RET_DK), F32),
            pltpu.VMEM((RET_HEADS, c, RET_DK), F32),
            pltpu.VMEM((RET_HEADS, RET_DK, RET_DV), F32),
        ],
        compiler_params=_cparams(("arbitrary", "arbitrary")),
        name="ret_fwd_scan",
    )(ld, proj, proj, proj)


def _ret_bwd_kernel(ld_ref, q_ref, k_ref, v_ref, gt_ref, o1_ref, x_ref, g1_ref, gn_ref, wo_ref, out_ref,
                    qd_scr, kd_scr, s_scr, og_scr):
    _ret_tables(ld_ref, pl.program_id(1) == 0, None, qd_scr, kd_scr, s_scr, backward=True)
    for h in range(RET_HEADS):
        _, _, _, cross = _ret_head_step(h, ld_ref[RET_HEADS + h], q_ref, k_ref, v_ref, qd_scr, kd_scr, s_scr)
        sl = slice(h * RET_DV, (h + 1) * RET_DV)
        o = o1_ref[:, sl] + cross
        mu = jnp.mean(o, axis=-1, keepdims=True)
        d = o - mu
        var = jnp.mean(d * d, axis=-1, keepdims=True)
        on = d * lax.rsqrt(var + EPS) * gn_ref[:, sl]
        og_scr[:, sl] = (on * jax.nn.silu(gt_ref[:, sl].astype(F32))).astype(BF16)
    mix = jnp.dot(og_scr[...], wo_ref[...], preferred_element_type=F32)
    out_ref[...] = x_ref[...] + g1_ref[0] * mix


def _ret_bwd(proj, o1, x, g1, gn_g, wo_bf, ld, batch, seq):
    c = RET_BLOCK
    nc = seq // c
    t = batch * seq
    rev = lambda b, i: b * nc + (nc - 1 - i)
    return pl.pallas_call(
        _ret_bwd_kernel,
        grid=(batch, nc),
        in_specs=[
            pl.BlockSpec(memory_space=pltpu.SMEM),
            pl.BlockSpec((c, RET_QK), lambda b, i: (rev(b, i), 0)),
            pl.BlockSpec((c, RET_QK), lambda b, i: (rev(b, i), 1)),
            pl.BlockSpec((c, RET_VDIM), lambda b, i: (rev(b, i), 1)),
            pl.BlockSpec((c, RET_VDIM), lambda b, i: (rev(b, i), 2)),
            pl.BlockSpec((c, RET_VDIM), lambda b, i: (rev(b, i), 0)),
            pl.BlockSpec((c, D_MODEL), lambda b, i: (rev(b, i), 0)),
            pl.BlockSpec((1, 1, D_MODEL), lambda b, i: (b, 0, 0)),
            pl.BlockSpec((1, RET_VDIM), lambda b, i: (0, 0)),
            pl.BlockSpec((RET_VDIM, D_MODEL), lambda b, i: (0, 0)),
        ],
        out_specs=pl.BlockSpec((c, D_MODEL), lambda b, i: (rev(b, i), 0)),
        out_shape=jax.ShapeDtypeStruct((t, D_MODEL), F32),
        scratch_shapes=[
            pltpu.VMEM((RET_HEADS, c, RET_DK), F32),
            pltpu.VMEM((RET_HEADS, c, RET_DK), F32),
            pltpu.VMEM((RET_HEADS, RET_DK, RET_DV), F32),
            pltpu.VMEM((c, RET_VDIM), BF16),
        ],
        compiler_params=_cparams(("arbitrary", "arbitrary")),
        name="ret_bwd_scan_out",
    )(ld, proj, proj, proj, proj, o1, x, g1, gn_g, wo_bf)


def _gmlp_kernel(x_ref, sc_ref, sh_ref, g1_ref, ng_ref, wi_ref, bi_ref, vg_ref, vb_ref, ws_ref, bs_ref,
                 wo_ref, bo_ref, out_ref, gated_scr, *, tm):
    x = x_ref[...]
    h = _norm_mod(x, ng_ref[...], sc_ref[0], sh_ref[0]).astype(BF16)
    z = jnp.dot(h, wi_ref[...], preferred_element_type=F32) + bi_ref[...]
    z = 0.5 * z * (1.0 + lax.erf(z * (2.0 ** -0.5)))
    u = z[:, :GM_WIDTH]
    v = z[:, GM_WIDTH:]
    mu = jnp.mean(v, axis=-1, keepdims=True)
    d = v - mu
    var = jnp.mean(d * d, axis=-1, keepdims=True)
    vn = (d * lax.rsqrt(var + EPS) * vg_ref[...] + vb_ref[...]).astype(BF16)
    for n in range(tm // GM_CHUNK):
        rows = slice(n * GM_CHUNK, (n + 1) * GM_CHUNK)
        for g in range(GM_GROUPS):
            cols = slice(g * GM_GDIM, (g + 1) * GM_GDIM)
            mixed = jnp.dot(ws_ref[g], vn[rows, cols], preferred_element_type=F32) + bs_ref[g]
            gated_scr[rows, cols] = (u[rows, cols] * mixed).astype(BF16)
    mix = jnp.dot(gated_scr[...], wo_ref[...], preferred_element_type=F32) + bo_ref[...]
    out_ref[...] = x + g1_ref[0] * mix


def _gmlp(x, sc, sh, g1, ng, wi_bf, bi, vg, vb, ws_bf, bs_col, wo_bf, bo, seq):
    t = x.shape[0]
    tm = 256
    spb = seq // tm
    const2 = lambda i: (0, 0)
    const3 = lambda i: (0, 0, 0)
    per_b = lambda i: (i // spb, 0, 0)
    return pl.pallas_call(
        functools.partial(_gmlp_kernel, tm=tm),
        grid=(t // tm,),
        in_specs=[
            pl.BlockSpec((tm, D_MODEL), lambda i: (i, 0)),
            pl.BlockSpec((1, 1, D_MODEL), per_b),
            pl.BlockSpec((1, 1, D_MODEL), per_b),
            pl.BlockSpec((1, 1, D_MODEL), per_b),
            pl.BlockSpec((1, D_MODEL), const2),
            pl.BlockSpec((D_MODEL, 2 * GM_WIDTH), const2),
            pl.BlockSpec((1, 2 * GM_WIDTH), const2),
            pl.BlockSpec((1, GM_WIDTH), const2),
            pl.BlockSpec((1, GM_WIDTH), const2),
            pl.BlockSpec((GM_GROUPS, GM_CHUNK, GM_CHUNK), const3),
            pl.BlockSpec((GM_GROUPS, GM_CHUNK, 1), const3),
            pl.BlockSpec((GM_WIDTH, D_MODEL), const2),
            pl.BlockSpec((1, D_MODEL), const2),
        ],
        out_specs=pl.BlockSpec((tm, D_MODEL), lambda i: (i, 0)),
        out_shape=jax.ShapeDtypeStruct((t, D_MODEL), F32),
        scratch_shapes=[pltpu.VMEM((tm, GM_WIDTH), BF16)],
        compiler_params=_cparams(("arbitrary",)),
        name="gmlp_mixer",
    )(x, sc, sh, g1, ng, wi_bf, bi, vg, vb, ws_bf, bs_col, wo_bf, bo)


def _router_kernel(x_ref, sc_ref, sh_ref, ng_ref, wr_ref, br_ref, h_ref, ti_ref, tg_ref, *, tm):
    h = _norm_mod(x_ref[...], ng_ref[...], sc_ref[0], sh_ref[0])
    h_ref[...] = h.astype(BF16)
    logits = jnp.dot(h, wr_ref[...], preferred_element_type=F32, precision=HIGHEST) + br_ref[...]
    lane_e = lax.broadcasted_iota(jnp.int32, (tm, N_EXPERTS), 1)
    lane = lax.broadcasted_iota(jnp.int32, (tm, LANES), 1)
    ti = jnp.zeros((tm, LANES), jnp.int32)
    tv = jnp.zeros((tm, LANES), F32)
    l = logits
    m0 = None
    for k in range(TOP_K):
        m = jnp.max(l, axis=-1, keepdims=True)
        idx = jnp.min(jnp.where(l == m, lane_e, N_EXPERTS), axis=-1, keepdims=True)
        l = jnp.where(lane_e == idx, -jnp.inf, l)
        if k == 0:
            m0 = m
        ti = jnp.where(lane == k, idx, ti)
        tv = jnp.where(lane == k, jnp.exp(m - m0), tv)
    ti_ref[...] = ti
    tg_ref[...] = tv / jnp.sum(tv, axis=-1, keepdims=True)


def _router(x, sc, sh, ng, w_r, b_r, seq):
    t = x.shape[0]
    tm = 512
    spb = seq // tm
    per_b = lambda i: (i // spb, 0, 0)
    return pl.pallas_call(
        functools.partial(_router_kernel, tm=tm),
        grid=(t // tm,),
        in_specs=[
            pl.BlockSpec((tm, D_MODEL), lambda i: (i, 0)),
            pl.BlockSpec((1, 1, D_MODEL), per_b),
            pl.BlockSpec((1, 1, D_MODEL), per_b),
            pl.BlockSpec((1, D_MODEL), lambda i: (0, 0)),
            pl.BlockSpec((D_MODEL, N_EXPERTS), lambda i: (0, 0)),
            pl.BlockSpec((1, N_EXPERTS), lambda i: (0, 0)),
        ],
        out_specs=[
            pl.BlockSpec((tm, D_MODEL), lambda i: (i, 0)),
            pl.BlockSpec((tm, LANES), lambda i: (i, 0)),
            pl.BlockSpec((tm, LANES), lambda i: (i, 0)),
        ],
        out_shape=[
            jax.ShapeDtypeStruct((t, D_MODEL), BF16),
            jax.ShapeDtypeStruct((t, LANES), jnp.int32),
            jax.ShapeDtypeStruct((t, LANES), F32),
        ],
        compiler_params=_cparams(("arbitrary",)),
        name="moe_norm_router",
    )(x, sc, sh, ng, w_r, b_r)


def _prep_wgu_kernel(w_ref, o_ref, *, tn):
    r = lax.broadcasted_iota(jnp.int32, (GU_GROUP, GU_GROUP), 0)
    c = lax.broadcasted_iota(jnp.int32, (GU_GROUP, GU_GROUP), 1)
    src = jnp.where(c < LANES, 2 * c, 2 * (c - LANES) + 1)
    perm = jnp.where(r == src, 1.0, 0.0).astype(BF16)
    w = w_ref[0].astype(BF16)
    for g in range(tn // GU_GROUP):
        sl = slice(g * GU_GROUP, (g + 1) * GU_GROUP)
        o_ref[0, :, sl] = jnp.dot(w[:, sl], perm, preferred_element_type=F32).astype(BF16)


def _prep_wgu(w_gu):
    ne = w_gu.shape[0]
    tn = 512
    return pl.pallas_call(
        functools.partial(_prep_wgu_kernel, tn=tn),
        grid=(ne, 2 * D_EXPERT // tn),
        in_specs=[pl.BlockSpec((1, D_MODEL, tn), lambda e, j: (e, 0, j))],
        out_specs=pl.BlockSpec((1, D_MODEL, tn), lambda e, j: (e, 0, j)),
        out_shape=jax.ShapeDtypeStruct(w_gu.shape, BF16),
        compiler_params=_cparams(("arbitrary", "arbitrary")),
        name="moe_prep_wgu",
    )(w_gu)


def _expert_kernel(be_ref, x_ref, wgu_ref, bgu_ref, wdn_ref, bdn_ref, y_ref):
    del be_ref
    hg = jnp.dot(x_ref[...], wgu_ref[0], preferred_element_type=F32) + bgu_ref[0]
    acts = []
    for g in range(2 * D_EXPERT // GU_GROUP):
        glu = jnp.minimum(hg[:, g * GU_GROUP:g * GU_GROUP + LANES], SWIGLU_LIMIT)
        lin = jnp.clip(hg[:, g * GU_GROUP + LANES:(g + 1) * GU_GROUP], -SWIGLU_LIMIT, SWIGLU_LIMIT)
        acts.append((glu * jax.nn.sigmoid(SWIGLU_ALPHA * glu) * (lin + 1.0)).astype(BF16))
    act = jnp.concatenate(acts, axis=-1)
    y = jnp.dot(act, wdn_ref[0], preferred_element_type=F32) + bdn_ref[0]
    y_ref[...] = y.astype(y_ref.dtype)


def _experts(xb, block_e, wgu_bf, bgu, wdn_bf, bdn):
    p = xb.shape[0]
    nb = p // MOE_BLOCK
    grid_spec = pltpu.PrefetchScalarGridSpec(
        num_scalar_prefetch=1,
        grid=(nb,),
        in_specs=[
            pl.BlockSpec((MOE_BLOCK, D_MODEL), lambda i, be: (i, 0)),
            pl.BlockSpec((1, D_MODEL, 2 * D_EXPERT), lambda i, be: (be[i], 0, 0)),
            pl.BlockSpec((1, 1, 2 * D_EXPERT), lambda i, be: (be[i], 0, 0)),
            pl.BlockSpec((1, D_EXPERT, D_MODEL), lambda i, be: (be[i], 0, 0)),
            pl.BlockSpec((1, 1, D_MODEL), lambda i, be: (be[i], 0, 0)),
        ],
        out_specs=pl.BlockSpec((MOE_BLOCK, D_MODEL), lambda i, be: (i, 0)),
    )
    return pl.pallas_call(
        _expert_kernel,
        grid_spec=grid_spec,
        out_shape=jax.ShapeDtypeStruct((p, D_MODEL), BF16),
        compiler_params=_cparams(("arbitrary",)),
        name="moe_experts",
    )(block_e, xb, wgu_bf, bgu, wdn_bf, bdn)


def _combine_kernel(yk_ref, tg_ref, x_ref, g2_ref, fg_ref, out_ref, *, final):
    tg = tg_ref[...]
    y = tg[:, 0:1] * yk_ref[0].astype(F32)
    for k in range(1, TOP_K):
        y = y + tg[:, k:k + 1] * yk_ref[k].astype(F32)
    xn = x_ref[...] + g2_ref[0] * y
    if final:
        xn = xn * lax.rsqrt(jnp.mean(xn * xn, axis=-1, keepdims=True) + EPS) * fg_ref[...]
    out_ref[...] = xn


def _combine(yk, tg, x, g2, final_g, seq, final):
    t = x.shape[0]
    tm = 512
    spb = seq // tm
    return pl.pallas_call(
        functools.partial(_combine_kernel, final=final),
        grid=(t // tm,),
        in_specs=[
            pl.BlockSpec((TOP_K, tm, D_MODEL), lambda i: (0, i, 0)),
            pl.BlockSpec((tm, LANES), lambda i: (i, 0)),
            pl.BlockSpec((tm, D_MODEL), lambda i: (i, 0)),
            pl.BlockSpec((1, 1, D_MODEL), lambda i: (i // spb, 0, 0)),
            pl.BlockSpec((1, D_MODEL), lambda i: (0, 0)),
        ],
        out_specs=pl.BlockSpec((tm, D_MODEL), lambda i: (i, 0)),
        out_shape=jax.ShapeDtypeStruct((t, D_MODEL), F32),
        compiler_params=_cparams(("arbitrary",)),
        name="moe_combine",
    )(yk, tg, x, g2, final_g)


def _moe_layer(x, sc2, sh2, g2, ng, w_r, b_r, wgu_bf, bgu, wdn_bf, bdn, final_g, seq, final):
    t = x.shape[0]
    h, ti, tg = _router(x, sc2, sh2, ng, w_r, b_r, seq)
    top_i = ti[:, :TOP_K]
    a = t * TOP_K
    flat_e = top_i.reshape(a)
    order = jnp.argsort(flat_e)
    se = flat_e[order]
    counts = jnp.bincount(flat_e, length=N_EXPERTS)
    padded = (counts + MOE_BLOCK - 1) // MOE_BLOCK * MOE_BLOCK
    start = jnp.cumsum(counts) - counts
    pend = jnp.cumsum(padded)
    pstart = pend - padded
    dest = (pstart[se] + (jnp.arange(a, dtype=jnp.int32) - start[se])).astype(jnp.int32)
    n_blocks = -(-a // MOE_BLOCK) + N_EXPERTS
    p = n_blocks * MOE_BLOCK
    row_tok = jnp.zeros((p,), jnp.int32).at[dest].set((order // TOP_K).astype(jnp.int32))
    pos = jnp.zeros((a,), jnp.int32).at[order].set(dest).reshape(t, TOP_K)
    block_e = jnp.minimum(
        jnp.searchsorted(pend, jnp.arange(n_blocks, dtype=jnp.int32) * MOE_BLOCK, side='right'),
        N_EXPERTS - 1).astype(jnp.int32)
    xb = h[row_tok]
    yb = _experts(xb, block_e, wgu_bf, bgu, wdn_bf, bdn)
    yk = yb[pos.T]
    return _combine(yk, tg, x, g2, final_g, seq, final)


def _rope_tables(seq):
    inv = ROPE_BASE ** (-jnp.arange(ROPE_HALF, dtype=F32) / ROPE_HALF)
    ang = jnp.arange(seq, dtype=F32)[:, None] * inv[None, :]
    return jnp.cos(ang), jnp.sin(ang)


def _trunk(x3, mod, wts):
    batch, seq, _ = x3.shape
    x = x3.reshape(batch * seq, D_MODEL)
    cos, sin = _rope_tables(seq)
    for i in range(DEPTH):
        sh1, sc1, g1, sh2, sc2, g2 = [m.reshape(batch, 1, D_MODEL) for m in jnp.split(mod[i], N_MOD, axis=-1)]
        if i == 0:
            proj = _inproj_ret(x, sc1, sh1, wts["norm1_g"][i], wts["ret_w_in"], cos, sin, seq)
            o1 = _ret_fwd(proj, wts["ret_ld"], batch, seq)
            x = _ret_bwd(proj, o1, x, g1, wts["ret_gn_g"], wts["ret_w_out"], wts["ret_ld"], batch, seq)
        else:
            x = _gmlp(x, sc1, sh1, g1, wts["norm1_g"][i], wts["gm_w_in"], wts["gm_b_in"], wts["gm_vn_g"],
                      wts["gm_vn_b"], wts["gm_w_s"], wts["gm_b_s"], wts["gm_w_out"], wts["gm_b_out"], seq)
        x = _moe_layer(x, sc2, sh2, g2, wts["norm2_g"][i], wts["moe_w_r"][i], wts["moe_b_r"][i],
                       wts["moe_w_gu"][i], wts["moe_b_gu"][i], wts["moe_w_dn"][i], wts["moe_b_dn"][i],
                       wts["final_g"], seq, final=(i == DEPTH - 1))
    return x.reshape(batch, seq, D_MODEL)


def kernel(x_prompt, x_sample, c_prompt, c_sample, ada_w, ada_b, norm1_g, norm2_g, ret_w_in, ret_log_decay,
           ret_gn_g, ret_w_out, gm_w_in, gm_b_in, gm_vn_g, gm_vn_b, gm_w_s, gm_b_s, gm_w_out, gm_b_out,
           moe_w_r, moe_b_r, moe_w_gu, moe_b_gu, moe_w_dn, moe_b_dn, final_g):
    bp = c_prompt.shape[0]
    bs = c_sample.shape[0]
    mod = _ada(jnp.concatenate([c_prompt, c_sample], axis=0), ada_w, ada_b)
    wgu = _prep_wgu(moe_w_gu.reshape(DEPTH * N_EXPERTS, D_MODEL, 2 * D_EXPERT)).reshape(
        DEPTH, N_EXPERTS, D_MODEL, 2 * D_EXPERT)
    bgu = moe_b_gu.reshape(DEPTH, N_EXPERTS, 2 * D_EXPERT // GU_GROUP, LANES, 2)
    bgu = jnp.swapaxes(bgu, -1, -2)
    wts = dict(
        norm1_g=norm1_g.reshape(DEPTH, 1, D_MODEL),
        norm2_g=norm2_g.reshape(DEPTH, 1, D_MODEL),
        ret_w_in=ret_w_in[0].astype(BF16),
        ret_ld=ret_log_decay[0].reshape(2 * RET_HEADS).astype(F32),
        ret_gn_g=ret_gn_g[0].reshape(1, RET_VDIM),
        ret_w_out=ret_w_out[0].astype(BF16),
        gm_w_in=gm_w_in[0].astype(BF16),
        gm_b_in=gm_b_in[0].reshape(1, 2 * GM_WIDTH),
        gm_vn_g=gm_vn_g[0].reshape(1, GM_WIDTH),
        gm_vn_b=gm_vn_b[0].reshape(1, GM_WIDTH),
        gm_w_s=gm_w_s[0].astype(BF16),
        gm_b_s=gm_b_s[0].reshape(GM_GROUPS, GM_CHUNK, 1),
        gm_w_out=gm_w_out[0].astype(BF16),
        gm_b_out=gm_b_out[0].reshape(1, D_MODEL),
        moe_w_r=moe_w_r,
        moe_b_r=moe_b_r.reshape(DEPTH, 1, N_EXPERTS),
        moe_w_gu=wgu,
        moe_b_gu=bgu.reshape(DEPTH, N_EXPERTS, 1, 2 * D_EXPERT),
        moe_w_dn=moe_w_dn.astype(BF16),
        moe_b_dn=moe_b_dn.reshape(DEPTH, N_EXPERTS, 1, D_MODEL),
        final_g=final_g.reshape(1, D_MODEL),
    )
    y_prompt = _trunk(x_prompt, mod[:, :bp], wts)
    y_sample = _trunk(x_sample, mod[:, bp:bp + bs], wts)
    return (y_prompt, y_sample)
```

```python
import functools

import jax
import jax.numpy as jnp
from jax import lax
from jax.experimental import pallas as pl
from jax.experimental.pallas import tpu as pltpu

F32 = jnp.float32
BF16 = jnp.bfloat16
HIGHEST = lax.Precision.HIGHEST

D_MODEL = 1024
DEPTH = 2
N_MOD = 6
EPS = 1e-6
RET_HEADS = 4
RET_DK = D_MODEL // RET_HEADS
RET_DV = 2 * RET_DK
RET_QK = RET_HEADS * RET_DK
RET_VDIM = RET_HEADS * RET_DV
RET_IN = 2 * RET_QK + 2 * RET_VDIM
ROPE_BASE = 10000.0
ROPE_HALF = RET_DK // 2
RET_BLOCK = 256
GM_GROUPS = 4
GM_WIDTH = 2 * D_MODEL
GM_GDIM = GM_WIDTH // GM_GROUPS
GM_CHUNK = 128
N_EXPERTS = 32
TOP_K = 4
D_EXPERT = D_MODEL
SWIGLU_LIMIT = 7.0
SWIGLU_ALPHA = 1.702
MOE_BLOCK = 256
LANES = 128
GU_GROUP = 2 * LANES
DISP_TILE = 256
SEG_SHIFT = 4
SEG_CH = 1 << SEG_SHIFT
PLANES = D_MODEL // 2 // LANES
COMB_ROWS = 1536

VMEM_LIMIT = 56 * 1024 * 1024


def _cparams(sem):
    return pltpu.CompilerParams(dimension_semantics=sem, vmem_limit_bytes=VMEM_LIMIT)


def _norm_mod(x, g, sc, sh):
    y = x * lax.rsqrt(jnp.mean(x * x, axis=-1, keepdims=True) + EPS) * g
    return y * (1.0 + sc) + sh


def _ada_kernel(c_ref, w_ref, b_ref, o_ref):
    ca = jax.nn.silu(c_ref[...])
    o_ref[0] = jnp.dot(ca, w_ref[0], preferred_element_type=F32, precision=HIGHEST) + b_ref[0]


def _ada(c_all, ada_w, ada_b):
    r = c_all.shape[0]
    tn = 768
    n = N_MOD * D_MODEL
    return pl.pallas_call(
        _ada_kernel,
        grid=(DEPTH, n // tn),
        in_specs=[
            pl.BlockSpec((r, D_MODEL), lambda l, j: (0, 0)),
            pl.BlockSpec((1, D_MODEL, tn), lambda l, j: (l, 0, j)),
            pl.BlockSpec((1, 1, tn), lambda l, j: (l, 0, j)),
        ],
        out_specs=pl.BlockSpec((1, r, tn), lambda l, j: (l, 0, j)),
        out_shape=jax.ShapeDtypeStruct((DEPTH, r, n), F32),
        compiler_params=_cparams(("arbitrary", "arbitrary")),
        name="ada_mod",
    )(c_all, ada_w, ada_b.reshape(DEPTH, 1, n))


def _inproj_ret_kernel(x_ref, sc_ref, sh_ref, g_ref, w_ref, cos_ref, sin_ref, o_ref, h_scr, *, tn):
    j = pl.program_id(1)
    n_q = RET_QK // tn
    n_rope = 2 * RET_QK // tn

    @pl.when(j == 0)
    def _():
        h_scr[...] = _norm_mod(x_ref[...], g_ref[...], sc_ref[0], sh_ref[0]).astype(BF16)

    acc = jnp.dot(h_scr[...], w_ref[...], preferred_element_type=F32)

    @pl.when(j < n_rope)
    def _():
        scale = jnp.where(j < n_q, RET_DK ** -0.5, 1.0).astype(F32)
        cos = cos_ref[...]
        sin = sin_ref[...]
        for hh in range(tn // RET_DK):
            lo = hh * RET_DK
            x1 = acc[:, lo:lo + ROPE_HALF]
            x2 = acc[:, lo + ROPE_HALF:lo + RET_DK]
            o_ref[:, lo:lo + ROPE_HALF] = ((x1 * cos - x2 * sin) * scale).astype(BF16)
            o_ref[:, lo + ROPE_HALF:lo + RET_DK] = ((x2 * cos + x1 * sin) * scale).astype(BF16)

    @pl.when(j >= n_rope)
    def _():
        o_ref[...] = acc.astype(BF16)


def _inproj_ret(x, sc, sh, g, w_bf, cos, sin, seq):
    t = x.shape[0]
    tm, tn = 512, 512
    spb = seq // tm
    return pl.pallas_call(
        functools.partial(_inproj_ret_kernel, tn=tn),
        grid=(t // tm, RET_IN // tn),
        in_specs=[
            pl.BlockSpec((tm, D_MODEL), lambda i, j: (i, 0)),
            pl.BlockSpec((1, 1, D_MODEL), lambda i, j: (i // spb, 0, 0)),
            pl.BlockSpec((1, 1, D_MODEL), lambda i, j: (i // spb, 0, 0)),
            pl.BlockSpec((1, D_MODEL), lambda i, j: (0, 0)),
            pl.BlockSpec((D_MODEL, tn), lambda i, j: (0, j)),
            pl.BlockSpec((tm, ROPE_HALF), lambda i, j: (i % spb, 0)),
            pl.BlockSpec((tm, ROPE_HALF), lambda i, j: (i % spb, 0)),
        ],
        out_specs=pl.BlockSpec((tm, tn), lambda i, j: (i, j)),
        out_shape=jax.ShapeDtypeStruct((t, RET_IN), BF16),
        scratch_shapes=[pltpu.VMEM((tm, D_MODEL), BF16)],
        compiler_params=_cparams(("arbitrary", "arbitrary")),
        name="ret_inproj",
    )(x, sc, sh, g, w_bf, cos, sin)


def _ret_tables(ld_ref, first, d_scr, qd_scr, kd_scr, s_scr, *, backward):
    c = RET_BLOCK

    @pl.when(first)
    def _():
        r = lax.broadcasted_iota(jnp.int32, (c, RET_DK), 0).astype(F32)
        if d_scr is not None:
            ri = lax.broadcasted_iota(jnp.int32, (c, c), 0).astype(F32)
            ci = lax.broadcasted_iota(jnp.int32, (c, c), 1).astype(F32)
            diff = ri - ci
        for h in range(RET_HEADS):
            lf = ld_ref[h]
            lb = ld_ref[RET_HEADS + h]
            if d_scr is not None:
                d_scr[h] = jnp.where(diff >= 0.0, jnp.exp(lf * jnp.maximum(diff, 0.0)),
                                     jnp.exp(lb * jnp.maximum(-diff, 0.0)))
            if backward:
                qd_scr[h] = jnp.exp(lb * (float(c) - r))
                kd_scr[h] = jnp.exp(lb * r)
            else:
                qd_scr[h] = jnp.exp(lf * (r + 1.0))
                kd_scr[h] = jnp.exp(lf * (float(c) - 1.0 - r))
            s_scr[h] = jnp.zeros((RET_DK, RET_DV), F32)


def _ret_head_step(h, ld, q_ref, k_ref, v_ref, qd_scr, kd_scr, s_scr):
    c = RET_BLOCK
    q = q_ref[:, h * RET_DK:(h + 1) * RET_DK]
    k = k_ref[:, h * RET_DK:(h + 1) * RET_DK]
    v = v_ref[:, h * RET_DV:(h + 1) * RET_DV]
    s = s_scr[h]
    qs = (q.astype(F32) * qd_scr[h]).astype(BF16)
    cross = jnp.dot(qs, s.astype(BF16), preferred_element_type=F32)
    ks = (k.astype(F32) * kd_scr[h]).astype(BF16)
    cd = jnp.exp(jnp.full((1, RET_DV), ld * float(c), F32))
    s_scr[h] = s * cd + lax.dot_general(ks, v, (((0,), (0,)), ((), ())), preferred_element_type=F32)
    return q, k, v, cross


def _ret_fwd_kernel(ld_ref, q_ref, k_ref, v_ref, o_ref, d_scr, qd_scr, kd_scr, s_scr):
    _ret_tables(ld_ref, pl.program_id(1) == 0, d_scr, qd_scr, kd_scr, s_scr, backward=False)
    for h in range(RET_HEADS):
        q, k, v, cross = _ret_head_step(h, ld_ref[h], q_ref, k_ref, v_ref, qd_scr, kd_scr, s_scr)
        s = lax.dot_general(q, k, (((1,), (1,)), ((), ())), preferred_element_type=F32) * d_scr[h]
        o_ref[:, h * RET_DV:(h + 1) * RET_DV] = jnp.dot(s.astype(BF16), v, preferred_element_type=F32) + cross


def _ret_fwd(proj, ld, batch, seq):
    c = RET_BLOCK
    nc = seq // c
    t = batch * seq
    return pl.pallas_call(
        _ret_fwd_kernel,
        grid=(batch, nc),
        in_specs=[
            pl.BlockSpec(memory_space=pltpu.SMEM),
            pl.BlockSpec((c, RET_QK), lambda b, i: (b * nc + i, 0)),
            pl.BlockSpec((c, RET_QK), lambda b, i: (b * nc + i, 1)),
            pl.BlockSpec((c, RET_VDIM), lambda b, i: (b * nc + i, 1)),
        ],
        out_specs=pl.BlockSpec((c, RET_VDIM), lambda b, i: (b * nc + i, 0)),
        out_shape=jax.ShapeDtypeStruct((t, RET_VDIM), F32),
        scratch_shapes=[
            pltpu.VMEM((RET_HEADS, c, c), F32),
            pltpu.VMEM((RET_HEADS, c, RET_DK), F32),
            pltpu.VMEM((RET_HEADS, c, RET_DK), F32),
            pltpu.VMEM((RET_HEADS, RET_DK, RET_DV), F32),
        ],
        compiler_params=_cparams(("arbitrary", "arbitrary")),
        name="ret_fwd_scan",
    )(ld, proj, proj, proj)


def _ret_bwd_kernel(ld_ref, q_ref, k_ref, v_ref, gt_ref, o1_ref, x_ref, g1_ref, gn_ref, wo_ref, out_ref,
                    qd_scr, kd_scr, s_scr, og_scr):
    _ret_tables(ld_ref, pl.program_id(1) == 0, None, qd_scr, kd_scr, s_scr, backward=True)
    for h in range(RET_HEADS):
        _, _, _, cross = _ret_head_step(h, ld_ref[RET_HEADS + h], q_ref, k_ref, v_ref, qd_scr, kd_scr, s_scr)
        sl = slice(h * RET_DV, (h + 1) * RET_DV)
        o = o1_ref[:, sl] + cross
        mu = jnp.mean(o, axis=-1, keepdims=True)
        d = o - mu
        var = jnp.mean(d * d, axis=-1, keepdims=True)
        on = d * lax.rsqrt(var + EPS) * gn_ref[:, sl]
        og_scr[:, sl] = (on * jax.nn.silu(gt_ref[:, sl].astype(F32))).astype(BF16)
    mix = jnp.dot(og_scr[...], wo_ref[...], preferred_element_type=F32)
    out_ref[...] = x_ref[...] + g1_ref[0] * mix


def _ret_bwd(proj, o1, x, g1, gn_g, wo_bf, ld, batch, seq):
    c = RET_BLOCK
    nc = seq // c
    t = batch * seq
    rev = lambda b, i: b * nc + (nc - 1 - i)
    return pl.pallas_call(
        _ret_bwd_kernel,
        grid=(batch, nc),
        in_specs=[
            pl.BlockSpec(memory_space=pltpu.SMEM),
            pl.BlockSpec((c, RET_QK), lambda b, i: (rev(b, i), 0)),
            pl.BlockSpec((c, RET_QK), lambda b, i: (rev(b, i), 1)),
            pl.BlockSpec((c, RET_VDIM), lambda b, i: (rev(b, i), 1)),
            pl.BlockSpec((c, RET_VDIM), lambda b, i: (rev(b, i), 2)),
            pl.BlockSpec((c, RET_VDIM), lambda b, i: (rev(b, i), 0)),
            pl.BlockSpec((c, D_MODEL), lambda b, i: (rev(b, i), 0)),
            pl.BlockSpec((1, 1, D_MODEL), lambda b, i: (b, 0, 0)),
            pl.BlockSpec((1, RET_VDIM), lambda b, i: (0, 0)),
            pl.BlockSpec((RET_VDIM, D_MODEL), lambda b, i: (0, 0)),
        ],
        out_specs=pl.BlockSpec((c, D_MODEL), lambda b, i: (rev(b, i), 0)),
        out_shape=jax.ShapeDtypeStruct((t, D_MODEL), F32),
        scratch_shapes=[
            pltpu.VMEM((RET_HEADS, c, RET_DK), F32),
            pltpu.VMEM((RET_HEADS, c, RET_DK), F32),
            pltpu.VMEM((RET_HEADS, RET_DK, RET_DV), F32),
            pltpu.VMEM((c, RET_VDIM), BF16),
        ],
        compiler_params=_cparams(("arbitrary", "arbitrary")),
        name="ret_bwd_scan_out",
    )(ld, proj, proj, proj, proj, o1, x, g1, gn_g, wo_bf)


def _gmlp_kernel(x_ref, sc_ref, sh_ref, g1_ref, ng_ref, wi_ref, bi_ref, vg_ref, vb_ref, ws_ref, bs_ref,
                 wo_ref, bo_ref, out_ref, gated_scr, *, tm):
    x = x_ref[...]
    h = _norm_mod(x, ng_ref[...], sc_ref[0], sh_ref[0]).astype(BF16)
    z = jnp.dot(h, wi_ref[...], preferred_element_type=F32) + bi_ref[...]
    z = 0.5 * z * (1.0 + lax.erf(z * (2.0 ** -0.5)))
    u = z[:, :GM_WIDTH]
    v = z[:, GM_WIDTH:]
    mu = jnp.mean(v, axis=-1, keepdims=True)
    d = v - mu
    var = jnp.mean(d * d, axis=-1, keepdims=True)
    vn = (d * lax.rsqrt(var + EPS) * vg_ref[...] + vb_ref[...]).astype(BF16)
    for n in range(tm // GM_CHUNK):
        rows = slice(n * GM_CHUNK, (n + 1) * GM_CHUNK)
        for g in range(GM_GROUPS):
            cols = slice(g * GM_GDIM, (g + 1) * GM_GDIM)
            mixed = jnp.dot(ws_ref[g], vn[rows, cols], preferred_element_type=F32) + bs_ref[g]
            gated_scr[rows, cols] = (u[rows, cols] * mixed).astype(BF16)
    mix = jnp.dot(gated_scr[...], wo_ref[...], preferred_element_type=F32) + bo_ref[...]
    out_ref[...] = x + g1_ref[0] * mix


def _gmlp(x, sc, sh, g1, ng, wi_bf, bi, vg, vb, ws_bf, bs_col, wo_bf, bo, seq):
    t = x.shape[0]
    tm = 256
    spb = seq // tm
    const2 = lambda i: (0, 0)
    const3 = lambda i: (0, 0, 0)
    per_b = lambda i: (i // spb, 0, 0)
    return pl.pallas_call(
        functools.partial(_gmlp_kernel, tm=tm),
        grid=(t // tm,),
        in_specs=[
            pl.BlockSpec((tm, D_MODEL), lambda i: (i, 0)),
            pl.BlockSpec((1, 1, D_MODEL), per_b),
            pl.BlockSpec((1, 1, D_MODEL), per_b),
            pl.BlockSpec((1, 1, D_MODEL), per_b),
            pl.BlockSpec((1, D_MODEL), const2),
            pl.BlockSpec((D_MODEL, 2 * GM_WIDTH), const2),
            pl.BlockSpec((1, 2 * GM_WIDTH), const2),
            pl.BlockSpec((1, GM_WIDTH), const2),
            pl.BlockSpec((1, GM_WIDTH), const2),
            pl.BlockSpec((GM_GROUPS, GM_CHUNK, GM_CHUNK), const3),
            pl.BlockSpec((GM_GROUPS, GM_CHUNK, 1), const3),
            pl.BlockSpec((GM_WIDTH, D_MODEL), const2),
            pl.BlockSpec((1, D_MODEL), const2),
        ],
        out_specs=pl.BlockSpec((tm, D_MODEL), lambda i: (i, 0)),
        out_shape=jax.ShapeDtypeStruct((t, D_MODEL), F32),
        scratch_shapes=[pltpu.VMEM((tm, GM_WIDTH), BF16)],
        compiler_params=_cparams(("arbitrary",)),
        name="gmlp_mixer",
    )(x, sc, sh, g1, ng, wi_bf, bi, vg, vb, ws_bf, bs_col, wo_bf, bo)


def _router_kernel(x_ref, sc_ref, sh_ref, ng_ref, wr_ref, br_ref, h_ref, tg_ref, cm_ref, rm_ref, tc_ref, tb_ref,
                   base_scr):
    tm = DISP_TILE

    @pl.when(pl.program_id(0) == 0)
    def _():
        base_scr[...] = jnp.zeros((1, LANES), F32)

    h = _norm_mod(x_ref[...], ng_ref[...], sc_ref[0], sh_ref[0])
    h_ref[...] = h.astype(BF16)
    logits = jnp.dot(h, wr_ref[...], preferred_element_type=F32, precision=HIGHEST) + br_ref[...]
    lane = lax.broadcasted_iota(jnp.int32, (tm, LANES), 1)
    l = logits
    ohs, es = [], []
    m0 = None
    for k in range(TOP_K):
        m = jnp.max(l, axis=-1, keepdims=True)
        idx = jnp.min(jnp.where(l == m, lane, LANES), axis=-1, keepdims=True)
        oh = lane == idx
        l = jnp.where(oh, -jnp.inf, l)
        if k == 0:
            m0 = m
        ohs.append(oh)
        es.append(jnp.exp(m - m0))
    denom = es[0] + es[1] + es[2] + es[3]
    sel = jnp.zeros((tm, LANES), F32)
    for oh in ohs:
        sel = jnp.where(oh, 1.0, sel)
    r = lax.broadcasted_iota(jnp.int32, (tm, tm), 0)
    c = lax.broadcasted_iota(jnp.int32, (tm, tm), 1)
    ltri = jnp.where(c < r, 1.0, 0.0).astype(BF16)
    prefix = jnp.dot(ltri, sel.astype(BF16), preferred_element_type=F32)
    cnt = jnp.sum(sel, axis=0, keepdims=True)
    er = lax.broadcasted_iota(jnp.int32, (LANES, LANES), 0)
    ec = lax.broadcasted_iota(jnp.int32, (LANES, LANES), 1)
    before = jnp.where(er < ec, 1.0, 0.0).astype(F32)
    loff = jnp.dot(cnt, before, preferred_element_type=F32, precision=HIGHEST)
    cnt_pad = jnp.ceil(cnt * (1.0 / SEG_CH)) * float(SEG_CH)
    loff_pad = jnp.dot(cnt_pad, before, preferred_element_type=F32, precision=HIGHEST)
    pos_d = prefix + loff
    pos_c = prefix + loff_pad
    tg = jnp.zeros((tm, LANES), F32)
    cm = jnp.zeros((tm, LANES), F32)
    dm = jnp.zeros((tm, LANES), F32)
    for k in range(TOP_K):
        tg = jnp.where(lane == k, es[k] / denom, tg)
        cm = jnp.where(lane == k, jnp.sum(jnp.where(ohs[k], pos_c, 0.0), axis=-1, keepdims=True), cm)
        dm = jnp.where(lane == k, jnp.sum(jnp.where(ohs[k], pos_d, 0.0), axis=-1, keepdims=True), dm)
    tg_ref[...] = tg
    cm_ref[...] = cm.astype(jnp.int32)
    rm_ref[...] = dm.T[:8, :].astype(jnp.int32)
    base = base_scr[...]
    tc_ref[0] = cnt.astype(jnp.int32)
    tb_ref[0] = base.astype(jnp.int32)
    base_scr[...] = base + cnt


def _router(x, sc, sh, ng, w_r, b_r, seq):
    t = x.shape[0]
    tm = DISP_TILE
    nt = t // tm
    spb = seq // tm
    per_b = lambda i: (i // spb, 0, 0)
    w_pad = jnp.pad(w_r, ((0, 0), (0, LANES - N_EXPERTS)))
    b_pad = jnp.concatenate([b_r, jnp.full((1, LANES - N_EXPERTS), -jnp.inf, F32)], axis=-1)
    return pl.pallas_call(
        _router_kernel,
        grid=(nt,),
        in_specs=[
            pl.BlockSpec((tm, D_MODEL), lambda i: (i, 0)),
            pl.BlockSpec((1, 1, D_MODEL), per_b),
            pl.BlockSpec((1, 1, D_MODEL), per_b),
            pl.BlockSpec((1, D_MODEL), lambda i: (0, 0)),
            pl.BlockSpec((D_MODEL, LANES), lambda i: (0, 0)),
            pl.BlockSpec((1, LANES), lambda i: (0, 0)),
        ],
        out_specs=[
            pl.BlockSpec((tm, D_MODEL), lambda i: (i, 0)),
            pl.BlockSpec((tm, LANES), lambda i: (i, 0)),
            pl.BlockSpec((tm, LANES), lambda i: (i, 0)),
            pl.BlockSpec((8, tm), lambda i: (0, i)),
            pl.BlockSpec((1, 1, LANES), lambda i: (i, 0, 0)),
            pl.BlockSpec((1, 1, LANES), lambda i: (i, 0, 0)),
        ],
        out_shape=[
            jax.ShapeDtypeStruct((t, D_MODEL), BF16),
            jax.ShapeDtypeStruct((t, LANES), F32),
            jax.ShapeDtypeStruct((t, LANES), jnp.int32),
            jax.ShapeDtypeStruct((8, t), jnp.int32),
            jax.ShapeDtypeStruct((nt, 1, LANES), jnp.int32),
            jax.ShapeDtypeStruct((nt, 1, LANES), jnp.int32),
        ],
        scratch_shapes=[pltpu.VMEM((1, LANES), F32)],
        compiler_params=_cparams(("arbitrary",)),
        name="moe_norm_router",
    )(x, sc, sh, ng, w_pad, b_pad)


def _prep_wgu_kernel(w_ref, o_ref, *, tn):
    r = lax.broadcasted_iota(jnp.int32, (GU_GROUP, GU_GROUP), 0)
    c = lax.broadcasted_iota(jnp.int32, (GU_GROUP, GU_GROUP), 1)
    src = jnp.where(c < LANES, 2 * c, 2 * (c - LANES) + 1)
    perm = jnp.where(r == src, 1.0, 0.0).astype(BF16)
    w = w_ref[0].astype(BF16)
    for g in range(tn // GU_GROUP):
        sl = slice(g * GU_GROUP, (g + 1) * GU_GROUP)
        o_ref[0, :, sl] = jnp.dot(w[:, sl], perm, preferred_element_type=F32).astype(BF16)


def _prep_wgu(w_gu):
    ne = w_gu.shape[0]
    tn = 512
    return pl.pallas_call(
        functools.partial(_prep_wgu_kernel, tn=tn),
        grid=(ne, 2 * D_EXPERT // tn),
        in_specs=[pl.BlockSpec((1, D_MODEL, tn), lambda e, j: (e, 0, j))],
        out_specs=pl.BlockSpec((1, D_MODEL, tn), lambda e, j: (e, 0, j)),
        out_shape=jax.ShapeDtypeStruct(w_gu.shape, BF16),
        compiler_params=_cparams(("arbitrary", "arbitrary")),
        name="moe_prep_wgu",
    )(w_gu)


def _pack_rows(y):
    half = D_MODEL // 2
    lo = lax.bitcast_convert_type(y[:, :half], jnp.uint32) >> 16
    hi = lax.bitcast_convert_type(y[:, half:], jnp.uint32) & jnp.uint32(0xFFFF0000)
    return hi | lo


def _unpack_rows(planes):
    los = [lax.bitcast_convert_type(w << 16, F32).astype(BF16) for w in planes]
    his = [lax.bitcast_convert_type(w & jnp.uint32(0xFFFF0000), F32).astype(BF16) for w in planes]
    return jnp.concatenate(los + his, axis=-1)


def _seg_chunks(n):
    return (n + (SEG_CH - 1)) >> SEG_SHIFT


def _dispatch_kernel(cnt_ref, base_ref, pstart_ref, zstart_ref, nz_ref, h_ref, rm_ref, xb_ref, stage, sem):
    i = pl.program_id(0)
    rows = TOP_K * DISP_TILE

    def chunk_copy(src_row, dst_row):
        return pltpu.make_async_copy(stage.at[:, pl.ds(src_row, SEG_CH), :],
                                     xb_ref.at[:, pl.ds(dst_row, SEG_CH), :], sem)

    def wait_chunks(n):
        def body(c, carry):
            chunk_copy(0, 0).wait()
            return carry
        lax.fori_loop(0, n, body, 0)

    @pl.when(i == 0)
    def _():
        stage[:, rows:, :] = jnp.zeros((PLANES, SEG_CH, LANES), jnp.uint32)

        def fill(e, total):
            nz = nz_ref[e]

            def body(c, carry):
                chunk_copy(rows, zstart_ref[e] + c * SEG_CH).start()
                return carry
            lax.fori_loop(0, nz, body, 0)
            return total + nz
        wait_chunks(lax.fori_loop(0, N_EXPERTS + 1, fill, jnp.int32(0)))

    lp = rm_ref[...]
    r = lax.broadcasted_iota(jnp.int32, (rows, DISP_TILE), 0)
    poh = jnp.zeros((rows, DISP_TILE), F32)
    for k in range(TOP_K):
        poh = jnp.where(r == lp[k:k + 1, :], 1.0, poh)
    w = _pack_rows(jnp.dot(poh.astype(BF16), h_ref[...], preferred_element_type=F32))
    for s in range(PLANES):
        stage[s, 0:rows, :] = w[:, s * LANES:(s + 1) * LANES]

    def per_expert(e, carry):
        src, total = carry
        n = cnt_ref[i * N_EXPERTS + e]
        dst = pstart_ref[e] + base_ref[i * N_EXPERTS + e]
        nch = _seg_chunks(n)

        def body(c, carry2):
            chunk_copy(src + c * SEG_CH, dst + c * SEG_CH).start()
            return carry2
        lax.fori_loop(0, nch, body, 0)
        return src + n, total + nch
    _, total = lax.fori_loop(0, N_EXPERTS, per_expert, (jnp.int32(0), jnp.int32(0)))
    wait_chunks(total)


def _dispatch(h, rm, cnt, base, pstart, zstart, nz, p_rows):
    t = h.shape[0]
    grid_spec = pltpu.PrefetchScalarGridSpec(
        num_scalar_prefetch=5,
        grid=(t // DISP_TILE,),
        in_specs=[
            pl.BlockSpec((DISP_TILE, D_MODEL), lambda i, *_: (i, 0)),
            pl.BlockSpec((8, DISP_TILE), lambda i, *_: (0, i)),
        ],
        out_specs=pl.BlockSpec(memory_space=pl.ANY),
        scratch_shapes=[
            pltpu.VMEM((PLANES, TOP_K * DISP_TILE + SEG_CH, LANES), jnp.uint32),
            pltpu.SemaphoreType.DMA(()),
        ],
    )
    return pl.pallas_call(
        _dispatch_kernel,
        grid_spec=grid_spec,
        out_shape=jax.ShapeDtypeStruct((PLANES, p_rows, LANES), jnp.uint32),
        compiler_params=_cparams(("arbitrary",)),
        name="moe_dispatch",
    )(cnt, base, pstart, zstart, nz, h, rm)


def _expert_kernel(be_ref, nused_ref, x_ref, wgu_ref, bgu_ref, wdn_ref, bdn_ref, y_ref):
    del be_ref
    used = pl.program_id(0) < nused_ref[0]

    @pl.when(used)
    def _():
        x = _unpack_rows([x_ref[s] for s in range(PLANES)])
        hg = jnp.dot(x, wgu_ref[0], preferred_element_type=F32) + bgu_ref[0]
        acts = []
        for g in range(2 * D_EXPERT // GU_GROUP):
            glu = jnp.minimum(hg[:, g * GU_GROUP:g * GU_GROUP + LANES], SWIGLU_LIMIT)
            lin = jnp.clip(hg[:, g * GU_GROUP + LANES:(g + 1) * GU_GROUP], -SWIGLU_LIMIT, SWIGLU_LIMIT)
            acts.append((glu * jax.nn.sigmoid(SWIGLU_ALPHA * glu) * (lin + 1.0)).astype(BF16))
        act = jnp.concatenate(acts, axis=-1)
        y = jnp.dot(act, wdn_ref[0], preferred_element_type=F32) + bdn_ref[0]
        w = _pack_rows(y.astype(BF16).astype(F32))
        for s in range(PLANES):
            y_ref[s] = w[:, s * LANES:(s + 1) * LANES]

    @pl.when(jnp.logical_not(used))
    def _():
        y_ref[...] = jnp.zeros(y_ref.shape, jnp.uint32)


def _experts(xb, block_e, n_used, wgu_bf, bgu, wdn_bf, bdn):
    p = xb.shape[1]
    nb = p // MOE_BLOCK
    grid_spec = pltpu.PrefetchScalarGridSpec(
        num_scalar_prefetch=2,
        grid=(nb,),
        in_specs=[
            pl.BlockSpec((PLANES, MOE_BLOCK, LANES), lambda i, be, nu: (0, i, 0)),
            pl.BlockSpec((1, D_MODEL, 2 * D_EXPERT), lambda i, be, nu: (be[i], 0, 0)),
            pl.BlockSpec((1, 1, 2 * D_EXPERT), lambda i, be, nu: (be[i], 0, 0)),
            pl.BlockSpec((1, D_EXPERT, D_MODEL), lambda i, be, nu: (be[i], 0, 0)),
            pl.BlockSpec((1, 1, D_MODEL), lambda i, be, nu: (be[i], 0, 0)),
        ],
        out_specs=pl.BlockSpec((PLANES, MOE_BLOCK, LANES), lambda i, be, nu: (0, i, 0)),
    )
    return pl.pallas_call(
        _expert_kernel,
        grid_spec=grid_spec,
        out_shape=jax.ShapeDtypeStruct(xb.shape, jnp.uint32),
        compiler_params=_cparams(("arbitrary",)),
        name="moe_experts",
    )(block_e, n_used, xb, wgu_bf, bgu, wdn_bf, bdn)


def _combine_kernel(cnt_ref, base_ref, pstart_ref, yb_ref, cm_ref, tg_ref, x_ref, g2_ref, fg_ref, out_ref,
                    stage, sem, *, final):
    i = pl.program_id(0)

    def chunk_copy(src_row, dst_row):
        return pltpu.make_async_copy(yb_ref.at[:, pl.ds(src_row, SEG_CH), :],
                                     stage.at[:, pl.ds(dst_row, SEG_CH), :], sem)

    @pl.when(i == 0)
    def _():
        stage[...] = jnp.zeros(stage.shape, jnp.uint32)

    def per_expert(e, carry):
        dst, total = carry
        n = cnt_ref[i * N_EXPERTS + e]
        src = pstart_ref[e] + base_ref[i * N_EXPERTS + e]
        nch = _seg_chunks(n)

        def body(c, carry2):
            chunk_copy(src + c * SEG_CH, dst + c * SEG_CH).start()
            return carry2
        lax.fori_loop(0, nch, body, 0)
        return dst + nch * SEG_CH, total + nch
    _, total = lax.fori_loop(0, N_EXPERTS, per_expert, (jnp.int32(0), jnp.int32(0)))

    cm = cm_ref[...]
    tg = tg_ref[...]
    lane = lax.broadcasted_iota(jnp.int32, (DISP_TILE, COMB_ROWS), 1)
    gmat = jnp.zeros((DISP_TILE, COMB_ROWS), F32)
    for k in range(TOP_K):
        gmat = jnp.where(lane == cm[:, k:k + 1], tg[:, k:k + 1], gmat)
    g_hi = gmat.astype(BF16)
    g_lo = (gmat - g_hi.astype(F32)).astype(BF16)

    def wait_body(c, carry):
        chunk_copy(0, 0).wait()
        return carry
    lax.fori_loop(0, total, wait_body, 0)

    yrows = _unpack_rows([stage[s] for s in range(PLANES)])
    y = (jnp.dot(g_hi, yrows, preferred_element_type=F32) + jnp.dot(g_lo, yrows, preferred_element_type=F32))
    xn = x_ref[...] + g2_ref[0] * y
    if final:
        xn = xn * lax.rsqrt(jnp.mean(xn * xn, axis=-1, keepdims=True) + EPS) * fg_ref[...]
    out_ref[...] = xn


def _combine(yb, cm, tg, x, g2, final_g, cnt, base, pstart, seq, final):
    t = x.shape[0]
    tm = DISP_TILE
    spb = seq // tm
    grid_spec = pltpu.PrefetchScalarGridSpec(
        num_scalar_prefetch=3,
        grid=(t // tm,),
        in_specs=[
            pl.BlockSpec(memory_space=pl.ANY),
            pl.BlockSpec((tm, LANES), lambda i, *_: (i, 0)),
            pl.BlockSpec((tm, LANES), lambda i, *_: (i, 0)),
            pl.BlockSpec((tm, D_MODEL), lambda i, *_: (i, 0)),
            pl.BlockSpec((1, 1, D_MODEL), lambda i, *_: (i // spb, 0, 0)),
            pl.BlockSpec((1, D_MODEL), lambda i, *_: (0, 0)),
        ],
        out_specs=pl.BlockSpec((tm, D_MODEL), lambda i, *_: (i, 0)),
        scratch_shapes=[
            pltpu.VMEM((PLANES, COMB_ROWS, LANES), jnp.uint32),
            pltpu.SemaphoreType.DMA(()),
        ],
    )
    return pl.pallas_call(
        functools.partial(_combine_kernel, final=final),
        grid_spec=grid_spec,
        out_shape=jax.ShapeDtypeStruct((t, D_MODEL), F32),
        compiler_params=_cparams(("arbitrary",)),
        name="moe_combine",
    )(cnt, base, pstart, yb, cm, tg, x, g2, final_g)


def _moe_layer(x, sc2, sh2, g2, ng, w_r, b_r, wgu_bf, bgu, wdn_bf, bdn, final_g, seq, final):
    t = x.shape[0]
    h, tg, cm, rm, tc, tb = _router(x, sc2, sh2, ng, w_r, b_r, seq)
    tc = tc[:, 0, :N_EXPERTS]
    tb = tb[:, 0, :N_EXPERTS]
    count = tb[-1] + tc[-1]
    region = (count + SEG_CH + MOE_BLOCK - 1) // MOE_BLOCK * MOE_BLOCK
    pend = jnp.cumsum(region)
    pstart = (pend - region).astype(jnp.int32)
    nb = -(-(t * TOP_K + N_EXPERTS * (SEG_CH + MOE_BLOCK - 1)) // MOE_BLOCK)
    block_e = jnp.minimum(
        jnp.searchsorted(pend, jnp.arange(nb, dtype=jnp.int32) * MOE_BLOCK, side='right'),
        N_EXPERTS - 1).astype(jnp.int32)
    n_used = (pend[-1:] // MOE_BLOCK).astype(jnp.int32)
    zstart = jnp.concatenate([pstart + count // SEG_CH * SEG_CH, pend[-1:]]).astype(jnp.int32)
    nz = ((jnp.concatenate([pend, jnp.full((1,), nb * MOE_BLOCK, pend.dtype)]) - zstart) // SEG_CH).astype(jnp.int32)
    cnt = tc.reshape(-1)
    base = tb.reshape(-1)
    xb = _dispatch(h, rm, cnt, base, pstart, zstart, nz, nb * MOE_BLOCK)
    yb = _experts(xb, block_e, n_used, wgu_bf, bgu, wdn_bf, bdn)
    return _combine(yb, cm, tg, x, g2, final_g, cnt, base, pstart, seq, final)


def _rope_tables(seq):
    inv = ROPE_BASE ** (-jnp.arange(ROPE_HALF, dtype=F32) / ROPE_HALF)
    ang = jnp.arange(seq, dtype=F32)[:, None] * inv[None, :]
    return jnp.cos(ang), jnp.sin(ang)


def _trunk(x3, mod, wts):
    batch, seq, _ = x3.shape
    x = x3.reshape(batch * seq, D_MODEL)
    cos, sin = _rope_tables(seq)
    for i in range(DEPTH):
        sh1, sc1, g1, sh2, sc2, g2 = [m.reshape(batch, 1, D_MODEL) for m in jnp.split(mod[i], N_MOD, axis=-1)]
        if i == 0:
            proj = _inproj_ret(x, sc1, sh1, wts["norm1_g"][i], wts["ret_w_in"], cos, sin, seq)
            o1 = _ret_fwd(proj, wts["ret_ld"], batch, seq)
            x = _ret_bwd(proj, o1, x, g1, wts["ret_gn_g"], wts["ret_w_out"], wts["ret_ld"], batch, seq)
        else:
            x = _gmlp(x, sc1, sh1, g1, wts["norm1_g"][i], wts["gm_w_in"], wts["gm_b_in"], wts["gm_vn_g"],
                      wts["gm_vn_b"], wts["gm_w_s"], wts["gm_b_s"], wts["gm_w_out"], wts["gm_b_out"], seq)
        x = _moe_layer(x, sc2, sh2, g2, wts["norm2_g"][i], wts["moe_w_r"][i], wts["moe_b_r"][i],
                       wts["moe_w_gu"][i], wts["moe_b_gu"][i], wts["moe_w_dn"][i], wts["moe_b_dn"][i],
                       wts["final_g"], seq, final=(i == DEPTH - 1))
    return x.reshape(batch, seq, D_MODEL)


def kernel(x_prompt, x_sample, c_prompt, c_sample, ada_w, ada_b, norm1_g, norm2_g, ret_w_in, ret_log_decay,
           ret_gn_g, ret_w_out, gm_w_in, gm_b_in, gm_vn_g, gm_vn_b, gm_w_s, gm_b_s, gm_w_out, gm_b_out,
           moe_w_r, moe_b_r, moe_w_gu, moe_b_gu, moe_w_dn, moe_b_dn, final_g):
    bp = c_prompt.shape[0]
    bs = c_sample.shape[0]
    mod = _ada(jnp.concatenate([c_prompt, c_sample], axis=0), ada_w, ada_b)
    wgu = _prep_wgu(moe_w_gu.reshape(DEPTH * N_EXPERTS, D_MODEL, 2 * D_EXPERT)).reshape(
        DEPTH, N_EXPERTS, D_MODEL, 2 * D_EXPERT)
    bgu = moe_b_gu.reshape(DEPTH, N_EXPERTS, 2 * D_EXPERT // GU_GROUP, LANES, 2)
    bgu = jnp.swapaxes(bgu, -1, -2)
    wts = dict(
        norm1_g=norm1_g.reshape(DEPTH, 1, D_MODEL),
        norm2_g=norm2_g.reshape(DEPTH, 1, D_MODEL),
        ret_w_in=ret_w_in[0].astype(BF16),
        ret_ld=ret_log_decay[0].reshape(2 * RET_HEADS).astype(F32),
        ret_gn_g=ret_gn_g[0].reshape(1, RET_VDIM),
        ret_w_out=ret_w_out[0].astype(BF16),
        gm_w_in=gm_w_in[0].astype(BF16),
        gm_b_in=gm_b_in[0].reshape(1, 2 * GM_WIDTH),
        gm_vn_g=gm_vn_g[0].reshape(1, GM_WIDTH),
        gm_vn_b=gm_vn_b[0].reshape(1, GM_WIDTH),
        gm_w_s=gm_w_s[0].astype(BF16),
        gm_b_s=gm_b_s[0].reshape(GM_GROUPS, GM_CHUNK, 1),
        gm_w_out=gm_w_out[0].astype(BF16),
        gm_b_out=gm_b_out[0].reshape(1, D_MODEL),
        moe_w_r=moe_w_r,
        moe_b_r=moe_b_r.reshape(DEPTH, 1, N_EXPERTS),
        moe_w_gu=wgu,
        moe_b_gu=bgu.reshape(DEPTH, N_EXPERTS, 1, 2 * D_EXPERT),
        moe_w_dn=moe_w_dn.astype(BF16),
        moe_b_dn=moe_b_dn.reshape(DEPTH, N_EXPERTS, 1, D_MODEL),
        final_g=final_g.reshape(1, D_MODEL),
    )
    y_prompt = _trunk(x_prompt, mod[:, :bp], wts)
    y_sample = _trunk(x_sample, mod[:, bp:bp + bs], wts)
    return (y_prompt, y_sample)
```
